```python
import jax, jax.numpy as jnp
from jax import lax
import numpy as np

D_MODEL = 1024
BATCH = 16
SEQ = 2048
DEPTH = 4
DEC_BATCH = 8
DEC_SEQ = 32
PAST_LEN = 4096

CHUNK = 64
N_MIXERS = 2
N_A_LAYERS = (DEPTH + 1) // 2
N_B_LAYERS = DEPTH // 2
M_HEADS = 4
M_DK = D_MODEL // 8
M_DV = D_MODEL // M_HEADS
M_PROJ = 2 * M_HEADS * M_DK + 2 * M_HEADS * M_DV + 2 * M_HEADS
A_HEADS = 16
A_KV_HEADS = 4
A_HD = D_MODEL // A_HEADS
A_GROUP = A_HEADS // A_KV_HEADS
WINDOW = 128
WINDOW_CHUNKS = WINDOW // CHUNK
SWA_ROWS = min(WINDOW, PAST_LEN)
ROPE_THETA = 10000.0
D_FF = 11 * D_MODEL // 4
FFN_RES = 0.5
EPS = 1e-6

kernel_name = "hybrid_mlstm_swa_streaming_step"


def rms_norm(x, g):
    xf = x.astype(jnp.float32)
    y = xf * lax.rsqrt(jnp.mean(xf * xf, axis=-1, keepdims=True) + EPS)
    return (y * g.astype(jnp.float32)).astype(x.dtype)


def swiglu(x, w_in, w_out):
    gate, up = jnp.split(x @ w_in, 2, axis=-1)
    return (jax.nn.silu(gate) * up) @ w_out


def half_ffn(x, g, w_in, w_out):
    return x + FFN_RES * swiglu(rms_norm(x, g), w_in, w_out)


def rope(x, pos):
    inv = ROPE_THETA ** (-jnp.arange(0, A_HD, 2, dtype=jnp.float32) / A_HD)
    ang = pos.astype(jnp.float32)[:, None] * inv[None, :]
    cos = jnp.cos(ang)[None, :, None, :]
    sin = jnp.sin(ang)[None, :, None, :]
    xf = x.astype(jnp.float32)
    x1, x2 = xf[..., : A_HD // 2], xf[..., A_HD // 2:]
    return jnp.concatenate([x1 * cos - x2 * sin, x2 * cos + x1 * sin], axis=-1).astype(x.dtype)


def mlstm_chunk_step(carry, inp):
    C, n, m = carry
    q, k, v, ig, lf = inp
    L = q.shape[2]
    b = jnp.cumsum(lf, axis=-1)
    causal = jnp.tril(jnp.ones((L, L), dtype=bool))
    d = jnp.where(causal, b[..., :, None] - b[..., None, :] + ig[..., None, :], -jnp.inf)
    g = b + m[..., None]
    m_t = jnp.maximum(g, jnp.max(d, axis=-1))
    w = jnp.exp(d - m_t[..., None]) * jnp.einsum('bhtd,bhsd->bhts', q, k)
    inter = jnp.exp(g - m_t)
    num = inter[..., None] * jnp.einsum('bhtd,bhde->bhte', q, C) + jnp.einsum('bhts,bhse->bhte', w, v)
    den = inter * jnp.einsum('bhtd,bhd->bht', q, n) + jnp.sum(w, axis=-1)
    h = num / jnp.maximum(jnp.abs(den), jnp.exp(-m_t))[..., None]
    b_last = b[..., -1]
    a = b_last[..., None] - b + ig
    m_new = jnp.maximum(m + b_last, jnp.max(a, axis=-1))
    decay = jnp.exp(m + b_last - m_new)
    wk = jnp.exp(a - m_new[..., None])[..., None] * k
    C_new = decay[..., None, None] * C + jnp.einsum('bhsd,bhse->bhde', wk, v)
    n_new = decay[..., None] * n + jnp.sum(wk, axis=2)
    return (C_new, n_new, m_new), h


def mlstm_scan(q, k, v, ig, lf, C0, n0, m0):
    B, T = q.shape[0], q.shape[1]
    L = min(CHUNK, T)
    nc = T // L
    blk = lambda a: a.reshape((B, nc, L) + a.shape[2:]).transpose((1, 0, 3, 2) + tuple(range(4, a.ndim + 1)))
    (C, n, m), h = lax.scan(mlstm_chunk_step, (C0, n0, m0),
                            (blk(q), blk(k), blk(v), blk(ig), blk(lf)))
    h = h.transpose(1, 0, 3, 2, 4).reshape(B, T, M_HEADS, M_DV)
    return h, C, n, m


def mlstm_mixer(x, w_in, b_gates, head_norm, w_out, C0, n0, m0):
    B, T, _ = x.shape
    f32 = jnp.float32
    hk, hv = M_HEADS * M_DK, M_HEADS * M_DV
    q, k, v, o, gates = jnp.split(x @ w_in, [hk, 2 * hk, 2 * hk + hv, 2 * hk + 2 * hv], axis=-1)
    q = q.reshape(B, T, M_HEADS, M_DK).astype(f32)
    k = k.reshape(B, T, M_HEADS, M_DK).astype(f32) * (M_DK ** -0.5)
    v = v.reshape(B, T, M_HEADS, M_DV).astype(f32)
    gates = gates.astype(f32) + b_gates.astype(f32)
    ig, fg = gates[..., :M_HEADS], gates[..., M_HEADS:]
    lf = jax.nn.log_sigmoid(fg)
    h, C, n, m = mlstm_scan(q, k, v, ig, lf, C0.astype(f32), n0.astype(f32), m0.astype(f32))
    h = h * lax.rsqrt(jnp.mean(h * h, axis=-1, keepdims=True) + EPS)
    h = h.reshape(B, T, hv) * head_norm.astype(f32)
    out = (jax.nn.sigmoid(o.astype(f32)) * h).astype(x.dtype) @ w_out
    return out, C, n, m


def swa_project(x, w_qkv, pos):
    B, T, _ = x.shape
    q, k, v = jnp.split(x @ w_qkv, [A_HEADS * A_HD, (A_HEADS + A_KV_HEADS) * A_HD], axis=-1)
    q = rope(q.reshape(B, T, A_HEADS, A_HD), pos)
    k = rope(k.reshape(B, T, A_KV_HEADS, A_HD), pos)
    v = v.reshape(B, T, A_KV_HEADS, A_HD)
    return q, k, v


def sink_softmax(s, sinks):
    sk = sinks[:, :, None, None]
    m = jnp.maximum(jnp.max(s, axis=-1, keepdims=True), sk)
    p = jnp.exp(s - m)
    return p / (jnp.sum(p, axis=-1, keepdims=True) + jnp.exp(sk - m))


def swa_prompt(x, w_qkv, sinks, w_out):
    B, T, _ = x.shape
    nc = T // CHUNK
    q, k, v = swa_project(x, w_qkv, jnp.arange(T))
    qc = q.reshape(B, nc, CHUNK, A_KV_HEADS, A_GROUP, A_HD)
    pad = ((0, 0), (WINDOW_CHUNKS, 0), (0, 0), (0, 0), (0, 0))
    kp = jnp.pad(k.reshape(B, nc, CHUNK, A_KV_HEADS, A_HD), pad)
    vp = jnp.pad(v.reshape(B, nc, CHUNK, A_KV_HEADS, A_HD), pad)
    kb = jnp.concatenate([kp[:, j:j + nc] for j in range(WINDOW_CHUNKS + 1)], axis=2)
    vb = jnp.concatenate([vp[:, j:j + nc] for j in range(WINDOW_CHUNKS + 1)], axis=2)
    chunk_idx = jnp.arange(nc)[:, None] - WINDOW_CHUNKS + jnp.arange(WINDOW_CHUNKS + 1)[None, :]
    valid = jnp.repeat(chunk_idx >= 0, CHUNK, axis=1)
    s = jnp.einsum('bnqkgd,bnskd->bnkgqs', qc, kb).astype(jnp.float32) * (A_HD ** -0.5)
    s = jnp.where(valid[None, :, None, None, None, :], s, -jnp.inf)
    p = sink_softmax(s, sinks.reshape(A_KV_HEADS, A_GROUP).astype(jnp.float32))
    o = jnp.einsum('bnkgqs,bnskd->bnqkgd', p.astype(vb.dtype), vb).reshape(B, T, A_HEADS * A_HD)
    keep = min(WINDOW, T)
    return o @ w_out, k[:, T - keep:], v[:, T - keep:]


def swa_sample(x, cache_k, cache_v, w_qkv, sinks, w_out):
    B, T, _ = x.shape
    q, k, v = swa_project(x, w_qkv, PAST_LEN + jnp.arange(T))
    ka = jnp.concatenate([cache_k.astype(k.dtype), k], axis=1)
    va = jnp.concatenate([cache_v.astype(v.dtype), v], axis=1)
    qg = q.reshape(B, T, A_KV_HEADS, A_GROUP, A_HD)
    s = jnp.einsum('bqkgd,bskd->bkgqs', qg, ka).astype(jnp.float32) * (A_HD ** -0.5)
    p = sink_softmax(s, sinks.reshape(A_KV_HEADS, A_GROUP).astype(jnp.float32))
    o = jnp.einsum('bkgqs,bskd->bqkgd', p.astype(va.dtype), va).reshape(B, T, A_HEADS * A_HD)
    return o @ w_out, k, v


def setup_inputs(seed: int = 0) -> dict:
    key = jax.random.key(seed)
    ks = jax.random.split(key, 24)
    nrm = lambda k, shape, scale: scale * jax.random.normal(k, shape, jnp.float32)
    gb = jnp.concatenate([nrm(ks[9], (N_A_LAYERS, M_HEADS), 0.1),
                          3.0 + nrm(ks[10], (N_A_LAYERS, M_HEADS), 0.5)], axis=-1)
    return {
        "x_prompt": nrm(ks[0], (BATCH, SEQ, D_MODEL), 1.0),
        "x_sample": nrm(ks[1], (DEC_BATCH, DEC_SEQ, D_MODEL), 1.0),
        "state_mlstm_C": nrm(ks[2], (N_A_LAYERS, DEC_BATCH, M_HEADS, M_DK, M_DV), 0.1),
        "state_mlstm_n": nrm(ks[3], (N_A_LAYERS, DEC_BATCH, M_HEADS, M_DK), 0.1),
        "state_mlstm_m": nrm(ks[4], (N_A_LAYERS, DEC_BATCH, M_HEADS), 1.0),
        "cache_swa_k": nrm(ks[5], (N_B_LAYERS, DEC_BATCH, SWA_ROWS, A_KV_HEADS, A_HD), 1.0),
        "cache_swa_v": nrm(ks[6], (N_B_LAYERS, DEC_BATCH, SWA_ROWS, A_KV_HEADS, A_HD), 1.0),
        "ffn_norm1": 1.0 + nrm(ks[7], (DEPTH, D_MODEL), 0.05),
        "ffn_w_in1": nrm(ks[8], (DEPTH, D_MODEL, 2 * D_FF), D_MODEL ** -0.5),
        "ffn_w_out1": nrm(ks[11], (DEPTH, D_FF, D_MODEL), D_FF ** -0.5),
        "mix_norm": 1.0 + nrm(ks[12], (DEPTH, D_MODEL), 0.05),
        "mlstm_w_in": nrm(ks[13], (N_A_LAYERS, D_MODEL, M_PROJ), D_MODEL ** -0.5),
        "mlstm_b_gates": gb,
        "mlstm_head_norm": 1.0 + nrm(ks[14], (N_A_LAYERS, M_HEADS * M_DV), 0.05),
        "mlstm_w_out": nrm(ks[15], (N_A_LAYERS, M_HEADS * M_DV, D_MODEL), (M_HEADS * M_DV) ** -0.5),
        "swa_w_qkv": nrm(ks[16], (N_B_LAYERS, D_MODEL, (A_HEADS + 2 * A_KV_HEADS) * A_HD), D_MODEL ** -0.5),
        "swa_sinks": nrm(ks[17], (N_B_LAYERS, A_HEADS), 0.5),
        "swa_w_out": nrm(ks[18], (N_B_LAYERS, A_HEADS * A_HD, D_MODEL), (A_HEADS * A_HD) ** -0.5),
        "ffn_norm2": 1.0 + nrm(ks[19], (DEPTH, D_MODEL), 0.05),
        "ffn_w_in2": nrm(ks[20], (DEPTH, D_MODEL, 2 * D_FF), D_MODEL ** -0.5),
        "ffn_w_out2": nrm(ks[21], (DEPTH, D_FF, D_MODEL), D_FF ** -0.5),
        "final_norm": 1.0 + nrm(ks[22], (D_MODEL,), 0.05),
    }


def reference(x_prompt, x_sample, state_mlstm_C, state_mlstm_n, state_mlstm_m, cache_swa_k, cache_swa_v,
              ffn_norm1, ffn_w_in1, ffn_w_out1, mix_norm, mlstm_w_in, mlstm_b_gates, mlstm_head_norm,
              mlstm_w_out, swa_w_qkv, swa_sinks, swa_w_out, ffn_norm2, ffn_w_in2, ffn_w_out2, final_norm):
    yp, ys = x_prompt, x_sample
    Bp = x_prompt.shape[0]
    pC, pn, pm, pk, pv = [], [], [], [], []
    sC, sn, sm, sk, sv = [], [], [], [], []
    for i in range(DEPTH):
        yp = half_ffn(yp, ffn_norm1[i], ffn_w_in1[i], ffn_w_out1[i])
        ys = half_ffn(ys, ffn_norm1[i], ffn_w_in1[i], ffn_w_out1[i])
        hp = rms_norm(yp, mix_norm[i])
        hs = rms_norm(ys, mix_norm[i])
        j = i // N_MIXERS
        if i % N_MIXERS == 0:
            C0 = jnp.zeros((Bp, M_HEADS, M_DK, M_DV), jnp.float32)
            n0 = jnp.zeros((Bp, M_HEADS, M_DK), jnp.float32)
            m0 = jnp.zeros((Bp, M_HEADS), jnp.float32)
            op, C, n, m = mlstm_mixer(hp, mlstm_w_in[j], mlstm_b_gates[j], mlstm_head_norm[j],
                                      mlstm_w_out[j], C0, n0, m0)
            os_, C2, n2, m2 = mlstm_mixer(hs, mlstm_w_in[j], mlstm_b_gates[j], mlstm_head_norm[j],
                                          mlstm_w_out[j], state_mlstm_C[j], state_mlstm_n[j], state_mlstm_m[j])
            pC.append(C); pn.append(n); pm.append(m)
            sC.append(C2); sn.append(n2); sm.append(m2)
        else:
            op, kk, vv = swa_prompt(hp, swa_w_qkv[j], swa_sinks[j], swa_w_out[j])
            os_, kn, vn = swa_sample(hs, cache_swa_k[j], cache_swa_v[j], swa_w_qkv[j], swa_sinks[j], swa_w_out[j])
            pk.append(kk); pv.append(vv)
            sk.append(kn); sv.append(vn)
        yp = yp + op
        ys = ys + os_
        yp = half_ffn(yp, ffn_norm2[i], ffn_w_in2[i], ffn_w_out2[i])
        ys = half_ffn(ys, ffn_norm2[i], ffn_w_in2[i], ffn_w_out2[i])
    y_prompt = rms_norm(yp, final_norm)
    y_sample = rms_norm(ys, final_norm)
    return (y_prompt, y_sample,
            jnp.stack(pC), jnp.stack(pn), jnp.stack(pm), jnp.stack(pk), jnp.stack(pv),
            jnp.stack(sC), jnp.stack(sn), jnp.stack(sm), jnp.stack(sk), jnp.stack(sv))
```

```python
import functools

import jax
import jax.numpy as jnp
from jax import lax
from jax.experimental import pallas as pl
from jax.experimental.pallas import tpu as pltpu

F32 = jnp.float32
BF16 = jnp.bfloat16

D_MODEL = 1024
DEPTH = 4
CHUNK = 64
M_HEADS = 4
M_DK = D_MODEL // 8
M_DV = D_MODEL // M_HEADS
M_QK = M_HEADS * M_DK
M_V = M_HEADS * M_DV
M_MAIN = 2 * M_QK + 2 * M_V
A_HEADS = 16
A_KV_HEADS = 4
A_HD = D_MODEL // A_HEADS
A_GROUP = A_HEADS // A_KV_HEADS
A_KV = A_KV_HEADS * A_HD
WINDOW = 128
PAST_LEN = 4096
ROPE_THETA = 10000.0
D_FF = 11 * D_MODEL // 4
FFN_RES = 0.5
EPS = 1e-6

LANES = 128
FFN_CHUNK = 256
PROJ_CHUNK = 512
ATTN_QB = 256
MIB = 1024 * 1024


def _rms(x, g):
    return x * lax.rsqrt(jnp.mean(x * x, axis=-1, keepdims=True) + EPS) * g


def _const_spec(shape):
    nd = len(shape)
    return pl.BlockSpec(shape, lambda *_: (0,) * nd, pipeline_mode=pl.Buffered(1))


def _params(n_grid, vmem_mib):
    return pltpu.CompilerParams(
        dimension_semantics=("arbitrary",) * n_grid,
        vmem_limit_bytes=vmem_mib * MIB)


def _ffn_body(x_ref, g_ref, win_ref, wout_ref, gf_ref, o_ref, h_ref, *, final):
    x = x_ref[...]
    xn = _rms(x, g_ref[...]).astype(BF16)
    for c in range(D_FF // FFN_CHUNK):
        lo = c * FFN_CHUNK
        gate = jnp.dot(xn, win_ref[:, lo:lo + FFN_CHUNK], preferred_element_type=F32)
        up = jnp.dot(xn, win_ref[:, D_FF + lo:D_FF + lo + FFN_CHUNK], preferred_element_type=F32)
        h_ref[:, lo:lo + FFN_CHUNK] = (gate * jax.nn.sigmoid(gate) * up).astype(BF16)
    y = x + FFN_RES * jnp.dot(h_ref[...], wout_ref[...], preferred_element_type=F32)
    if final:
        y = _rms(y, gf_ref[...])
    o_ref[...] = y


def _ffn(x, g, w_in, w_out, g_final, *, tm, final):
    n = x.shape[0]
    return pl.pallas_call(
        functools.partial(_ffn_body, final=final),
        grid=(n // tm,),
        in_specs=[
            pl.BlockSpec((tm, D_MODEL), lambda i: (i, 0)),
            _const_spec((1, D_MODEL)),
            _const_spec((D_MODEL, 2 * D_FF)),
            _const_spec((D_FF, D_MODEL)),
            _const_spec((1, D_MODEL)),
        ],
        out_specs=pl.BlockSpec((tm, D_MODEL), lambda i: (i, 0)),
        out_shape=jax.ShapeDtypeStruct((n, D_MODEL), F32),
        scratch_shapes=[pltpu.VMEM((tm, D_FF), BF16)],
        compiler_params=_params(1, 48),
        name="half_ffn",
    )(x, g, w_in, w_out, g_final)


def _mproj_body(x_ref, g_ref, w_ref, wg_ref, bg_ref, o_ref, og_ref):
    xn = _rms(x_ref[...], g_ref[...]).astype(BF16)
    for c in range(M_MAIN // PROJ_CHUNK):
        lo = c * PROJ_CHUNK
        o_ref[:, lo:lo + PROJ_CHUNK] = jnp.dot(
            xn, w_ref[:, lo:lo + PROJ_CHUNK], preferred_element_type=F32).astype(BF16)
    og_ref[...] = jnp.dot(xn, wg_ref[...], preferred_element_type=F32) + bg_ref[...]


def _mproj(x, g, w, wg, bg, *, tm):
    n = x.shape[0]
    return pl.pallas_call(
        _mproj_body,
        grid=(n // tm,),
        in_specs=[
            pl.BlockSpec((tm, D_MODEL), lambda i: (i, 0)),
            _const_spec((1, D_MODEL)),
            _const_spec((D_MODEL, M_MAIN)),
            _const_spec((D_MODEL, LANES)),
            _const_spec((1, LANES)),
        ],
        out_specs=[
            pl.BlockSpec((tm, M_MAIN), lambda i: (i, 0)),
            pl.BlockSpec((tm, LANES), lambda i: (i, 0)),
        ],
        out_shape=[
            jax.ShapeDtypeStruct((n, M_MAIN), BF16),
            jax.ShapeDtypeStruct((n, LANES), F32),
        ],
        compiler_params=_params(1, 40),
        name="mlstm_in_proj",
    )(x, g, w, wg, bg)


def _mlstm_body(qkvo_ref, gates_ref, c0_ref, n0_ref, m0_ref, hn_ref,
                hg_ref, c_ref, n_ref, m_ref, *, bg, L):
    @pl.when(pl.program_id(1) == 0)
    def _():
        c_ref[...] = c0_ref[...]
        n_ref[...] = n0_ref[...]
        m_ref[...] = m0_ref[...]

    scale = M_DK ** -0.5
    row = lax.broadcasted_iota(jnp.int32, (L, L), 0)
    col = lax.broadcasted_iota(jnp.int32, (L, L), 1)
    causal = col <= row
    tril = causal.astype(F32)
    lane = lax.broadcasted_iota(jnp.int32, (L, LANES), 1)
    pad = jnp.zeros((LANES - L, LANES), F32)

    for b in range(bg):
        gts = gates_ref[b]
        lf = jax.nn.log_sigmoid(gts)
        bcs = jnp.dot(tril, lf, precision=lax.Precision.HIGHEST, preferred_element_type=F32)
        both = jnp.where(lane < M_HEADS, gts, bcs)
        both_t = jnp.concatenate([both, pad], axis=0).T
        for h in range(M_HEADS):
            ig_row = both_t[h:h + 1, 0:L]
            b_row = both_t[M_HEADS + h:M_HEADS + h + 1, 0:L]
            ig_col = gts[:, h:h + 1]
            b_col = bcs[:, M_HEADS + h:M_HEADS + h + 1]
            m_prev = m_ref[b, h:h + 1, 0:1]
            n_row = n_ref[b, h:h + 1, :]
            c_prev = c_ref[b, h]

            q = qkvo_ref[b, :, h * M_DK:(h + 1) * M_DK]
            k = qkvo_ref[b, :, M_QK + h * M_DK:M_QK + (h + 1) * M_DK]
            v = qkvo_ref[b, :, 2 * M_QK + h * M_DV:2 * M_QK + (h + 1) * M_DV]
            og = qkvo_ref[b, :, 2 * M_QK + M_V + h * M_DV:2 * M_QK + M_V + (h + 1) * M_DV]

            d = jnp.where(causal, b_col - b_row + ig_row, -jnp.inf)
            g_col = b_col + m_prev
            m_t = jnp.maximum(g_col, jnp.max(d, axis=-1, keepdims=True))
            s = lax.dot_general(q, k, (((1,), (1,)), ((), ())), preferred_element_type=F32) * scale
            w = jnp.exp(d - m_t) * s
            inter = jnp.exp(g_col - m_t)
            qc = jnp.dot(q, c_prev.astype(BF16), preferred_element_type=F32)
            wv = jnp.dot(w.astype(BF16), v, preferred_element_type=F32)
            num = inter * qc + wv
            qn = jnp.sum(q.astype(F32) * n_row, axis=-1, keepdims=True)
            den = inter * qn + jnp.sum(w, axis=-1, keepdims=True)
            hh = num / jnp.maximum(jnp.abs(den), jnp.exp(-m_t))
            hh = hh * lax.rsqrt(jnp.mean(hh * hh, axis=-1, keepdims=True) + EPS)
            hh = hh * hn_ref[:, h * M_DV:(h + 1) * M_DV]
            hg_ref[b, :, h * M_DV:(h + 1) * M_DV] = (jax.nn.sigmoid(og.astype(F32)) * hh).astype(BF16)

            b_last = b_row[:, L - 1:L]
            a_row = b_last - b_row + ig_row
            a_col = b_last - b_col + ig_col
            m_new = jnp.maximum(m_prev + b_last, jnp.max(a_row, axis=-1, keepdims=True))
            decay = jnp.exp(m_prev + b_last - m_new)
            wk = (jnp.exp(a_col - m_new) * scale) * k.astype(F32)
            kv = lax.dot_general(wk.astype(BF16), v, (((0,), (0,)), ((), ())),
                                 preferred_element_type=F32)
            c_ref[b, h] = decay * c_prev + kv
            n_ref[b, h:h + 1, :] = decay * n_row + jnp.sum(wk, axis=0, keepdims=True)
            m_ref[b, h:h + 1, :] = jnp.broadcast_to(m_new, (1, LANES))


def _mlstm(qkvo, gates, c0, n0, m0, hnorm, *, bg, L):
    nb, t, _ = qkvo.shape
    state_specs = [
        pl.BlockSpec((bg, M_HEADS, M_DK, M_DV), lambda g, c: (g, 0, 0, 0)),
        pl.BlockSpec((bg, M_HEADS, M_DK), lambda g, c: (g, 0, 0)),
        pl.BlockSpec((bg, M_HEADS, LANES), lambda g, c: (g, 0, 0)),
    ]
    return pl.pallas_call(
        functools.partial(_mlstm_body, bg=bg, L=L),
        grid=(nb // bg, t // L),
        in_specs=[
            pl.BlockSpec((bg, L, M_MAIN), lambda g, c: (g, c, 0)),
            pl.BlockSpec((bg, L, LANES), lambda g, c: (g, c, 0)),
            *state_specs,
            _const_spec((1, M_V)),
        ],
        out_specs=[pl.BlockSpec((bg, L, M_V), lambda g, c: (g, c, 0)), *state_specs],
        out_shape=[
            jax.ShapeDtypeStruct((nb, t, M_V), BF16),
            jax.ShapeDtypeStruct((nb, M_HEADS, M_DK, M_DV), F32),
            jax.ShapeDtypeStruct((nb, M_HEADS, M_DK), F32),
            jax.ShapeDtypeStruct((nb, M_HEADS, LANES), F32),
        ],
        compiler_params=_params(2, 32),
        name="mlstm_scan",
    )(qkvo, gates, c0, n0, m0, hnorm)


def _rope(x, cos, sin_signed, first_half):
    swapped = jnp.where(first_half, pltpu.roll(x, LANES - A_HD // 2, 1), pltpu.roll(x, A_HD // 2, 1))
    return x * cos + swapped * sin_signed


def _sproj_body(x_ref, g_ref, wq_ref, wkv_ref, cos_ref, sin_ref, q_ref, kv_ref):
    xn = _rms(x_ref[...], g_ref[...]).astype(BF16)
    cos = cos_ref[...]
    sin_signed = sin_ref[...]
    lane = lax.broadcasted_iota(jnp.int32, cos.shape, 1)
    first_half = (lane & (A_HD // 2)) == 0
    q_scale = A_HD ** -0.5
    for c in range(D_MODEL // PROJ_CHUNK):
        q = jnp.dot(xn, wq_ref[:, c * PROJ_CHUNK:(c + 1) * PROJ_CHUNK], preferred_element_type=F32)
        for j in range(PROJ_CHUNK // LANES):
            lo = c * PROJ_CHUNK + j * LANES
            blk = _rope(q[:, j * LANES:(j + 1) * LANES], cos, sin_signed, first_half)
            q_ref[:, lo:lo + LANES] = (blk * q_scale).astype(BF16)
    kv = jnp.dot(xn, wkv_ref[...], preferred_element_type=F32)
    for j in range(A_KV // LANES):
        kv_ref[:, j * LANES:(j + 1) * LANES] = _rope(
            kv[:, j * LANES:(j + 1) * LANES], cos, sin_signed, first_half)
    kv_ref[:, A_KV:2 * A_KV] = kv[:, A_KV:2 * A_KV]


def _sproj(x, g, wq, wkv, cos, sin_signed, *, tm):
    n = x.shape[0]
    n_tab = cos.shape[0] // tm
    return pl.pallas_call(
        _sproj_body,
        grid=(n // tm,),
        in_specs=[
            pl.BlockSpec((tm, D_MODEL), lambda i: (i, 0)),
            _const_spec((1, D_MODEL)),
            _const_spec((D_MODEL, D_MODEL)),
            _const_spec((D_MODEL, 2 * A_KV)),
            pl.BlockSpec((tm, LANES), lambda i: (i % n_tab, 0)),
            pl.BlockSpec((tm, LANES), lambda i: (i % n_tab, 0)),
        ],
        out_specs=[
            pl.BlockSpec((tm, D_MODEL), lambda i: (i, 0)),
            pl.BlockSpec((tm, 2 * A_KV), lambda i: (i, 0)),
        ],
        out_shape=[
            jax.ShapeDtypeStruct((n, D_MODEL), BF16),
            jax.ShapeDtypeStruct((n, 2 * A_KV), F32),
        ],
        compiler_params=_params(1, 32),
        name="swa_in_proj",
    )(x, g, wq, wkv, cos, sin_signed)


def _attn_group(sinks_ref, q_ref, o_ref, kv_scr, *, q_lo, q_rows, k_lo, k_rows, g, first_valid):
    kb = kv_scr[k_lo:k_lo + k_rows, g * A_HD:(g + 1) * A_HD]
    vb = kv_scr[k_lo:k_lo + k_rows, A_KV + g * A_HD:A_KV + (g + 1) * A_HD]
    heads = [A_GROUP * g + j for j in range(A_GROUP)]
    qs = jnp.concatenate(
        [q_ref[0, q_lo:q_lo + q_rows, hd * A_HD:(hd + 1) * A_HD] for hd in heads], axis=0)
    s = lax.dot_general(qs, kb, (((1,), (1,)), ((), ())), preferred_element_type=F32)
    if first_valid is not None:
        key = lax.broadcasted_iota(jnp.int32, s.shape, 1)
        s = jnp.where(key >= first_valid, s, -jnp.inf)
    sk = jnp.concatenate([jnp.full((q_rows, 1), sinks_ref[hd], F32) for hd in heads], axis=0)
    m = jnp.maximum(jnp.max(s, axis=-1, keepdims=True), sk)
    p = jnp.exp(s - m)
    den = jnp.sum(p, axis=-1, keepdims=True) + jnp.exp(sk - m)
    o = jnp.dot(p.astype(BF16), vb, preferred_element_type=F32) / den
    for j, hd in enumerate(heads):
        o_ref[0, q_lo:q_lo + q_rows, hd * A_HD:(hd + 1) * A_HD] = (
            o[j * q_rows:(j + 1) * q_rows].astype(BF16))


def _attn_prompt_body(sinks_ref, q_ref, kvp_ref, kvo_ref, o_ref, kv_scr):
    kv_scr[0:WINDOW, :] = kvp_ref[0].astype(BF16)
    kv_scr[WINDOW:WINDOW + ATTN_QB, :] = kvo_ref[0].astype(BF16)
    band = WINDOW + CHUNK
    n_missing = jnp.where(pl.program_id(1) == 0, WINDOW, 0)
    for i in range(ATTN_QB // CHUNK):
        for g in range(A_KV_HEADS):
            first_valid = (n_missing - i * CHUNK) if i * CHUNK < WINDOW else None
            _attn_group(sinks_ref, q_ref, o_ref, kv_scr, q_lo=i * CHUNK, q_rows=CHUNK,
                        k_lo=i * CHUNK, k_rows=band, g=g, first_valid=first_valid)


def _attn_prompt(q, kv, sinks):
    nb, t, _ = q.shape
    per_prev = ATTN_QB // WINDOW
    return pl.pallas_call(
        _attn_prompt_body,
        grid=(nb, t // ATTN_QB),
        in_specs=[
            pl.BlockSpec(memory_space=pltpu.SMEM),
            pl.BlockSpec((1, ATTN_QB, D_MODEL), lambda b, i: (b, i, 0)),
            pl.BlockSpec((1, WINDOW, 2 * A_KV), lambda b, i: (b, jnp.maximum(i * per_prev - 1, 0), 0)),
            pl.BlockSpec((1, ATTN_QB, 2 * A_KV), lambda b, i: (b, i, 0)),
        ],
        out_specs=pl.BlockSpec((1, ATTN_QB, D_MODEL), lambda b, i: (b, i, 0)),
        out_shape=jax.ShapeDtypeStruct((nb, t, D_MODEL), BF16),
        scratch_shapes=[pltpu.VMEM((WINDOW + ATTN_QB, 2 * A_KV), BF16)],
        compiler_params=_params(2, 32),
        name="swa_prompt_attn",
    )(sinks, q, kv, kv)


def _attn_sample_body(sinks_ref, q_ref, ck_ref, cv_ref, kv_ref, o_ref, kv_scr):
    rows = ck_ref.shape[1]
    t = q_ref.shape[1]
    kv_scr[0:rows, 0:A_KV] = ck_ref[0].astype(BF16)
    kv_scr[0:rows, A_KV:2 * A_KV] = cv_ref[0].astype(BF16)
    kv_scr[rows:rows + t, :] = kv_ref[0].astype(BF16)
    for g in range(A_KV_HEADS):
        _attn_group(sinks_ref, q_ref, o_ref, kv_scr, q_lo=0, q_rows=t,
                    k_lo=0, k_rows=rows + t, g=g, first_valid=None)


def _attn_sample(q, cache_k, cache_v, kv, sinks):
    nb, t, _ = q.shape
    rows = cache_k.shape[1]
    return pl.pallas_call(
        _attn_sample_body,
        grid=(nb,),
        in_specs=[
            pl.BlockSpec(memory_space=pltpu.SMEM),
            pl.BlockSpec((1, t, D_MODEL), lambda b: (b, 0, 0)),
            pl.BlockSpec((1, rows, A_KV), lambda b: (b, 0, 0)),
            pl.BlockSpec((1, rows, A_KV), lambda b: (b, 0, 0)),
            pl.BlockSpec((1, t, 2 * A_KV), lambda b: (b, 0, 0)),
        ],
        out_specs=pl.BlockSpec((1, t, D_MODEL), lambda b: (b, 0, 0)),
        out_shape=jax.ShapeDtypeStruct((nb, t, D_MODEL), BF16),
        scratch_shapes=[pltpu.VMEM((rows + t, 2 * A_KV), BF16)],
        compiler_params=_params(1, 32),
        name="swa_sample_attn",
    )(sinks, q, cache_k, cache_v, kv)


def _oproj_body(x_ref, a_ref, w_ref, o_ref):
    o_ref[...] = x_ref[...] + jnp.dot(a_ref[...], w_ref[...], preferred_element_type=F32)


def _oproj(x, a, w, *, tm):
    n = x.shape[0]
    return pl.pallas_call(
        _oproj_body,
        grid=(n // tm,),
        in_specs=[
            pl.BlockSpec((tm, D_MODEL), lambda i: (i, 0)),
            pl.BlockSpec((tm, D_MODEL), lambda i: (i, 0)),
            _const_spec((D_MODEL, D_MODEL)),
        ],
        out_specs=pl.BlockSpec((tm, D_MODEL), lambda i: (i, 0)),
        out_shape=jax.ShapeDtypeStruct((n, D_MODEL), F32),
        compiler_params=_params(1, 32),
        name="mixer_out_proj",
    )(x, a, w)


def _rope_tables(pos):
    inv = ROPE_THETA ** (-jnp.arange(0, A_HD, 2, dtype=F32) / A_HD)
    ang = pos.astype(F32)[:, None] * inv[None, :]
    cos = jnp.cos(ang)
    sin = jnp.sin(ang)
    reps = LANES // A_HD
    return (jnp.concatenate([cos, cos] * reps, axis=-1),
            jnp.concatenate([-sin, sin] * reps, axis=-1))


def kernel(x_prompt, x_sample, state_mlstm_C, state_mlstm_n, state_mlstm_m, cache_swa_k, cache_swa_v,
           ffn_norm1, ffn_w_in1, ffn_w_out1, mix_norm, mlstm_w_in, mlstm_b_gates, mlstm_head_norm,
           mlstm_w_out, swa_w_qkv, swa_sinks, swa_w_out, ffn_norm2, ffn_w_in2, ffn_w_out2, final_norm):
    bp, tp, _ = x_prompt.shape
    bs, ts, _ = x_sample.shape
    tm_p = 512
    tm_s = bs * ts

    w_in1, w_out1 = ffn_w_in1.astype(BF16), ffn_w_out1.astype(BF16)
    w_in2, w_out2 = ffn_w_in2.astype(BF16), ffn_w_out2.astype(BF16)
    m_w_main = mlstm_w_in[:, :, :M_MAIN].astype(BF16)
    n_gates = 2 * M_HEADS
    m_w_gates = jnp.pad(mlstm_w_in[:, :, M_MAIN:], ((0, 0), (0, 0), (0, LANES - n_gates))).astype(BF16)
    m_b_gates = jnp.pad(mlstm_b_gates.astype(F32), ((0, 0), (0, LANES - n_gates)))[:, None, :]
    m_w_out = mlstm_w_out.astype(BF16)
    s_wq = swa_w_qkv[:, :, :D_MODEL].astype(BF16)
    s_wkv = swa_w_qkv[:, :, D_MODEL:].astype(BF16)
    s_w_out = swa_w_out.astype(BF16)
    sinks = swa_sinks.astype(F32)

    cos_p, sin_p = _rope_tables(jnp.arange(tp))
    cos_s, sin_s = _rope_tables(PAST_LEN + jnp.arange(ts))
    cos_s, sin_s = jnp.tile(cos_s, (bs, 1)), jnp.tile(sin_s, (bs, 1))

    row = lambda a: a.astype(F32)[None, :]
    yp = x_prompt.reshape(bp * tp, D_MODEL)
    ys = x_sample.reshape(bs * ts, D_MODEL)
    zc = jnp.zeros((bp, M_HEADS, M_DK, M_DV), F32)
    zn = jnp.zeros((bp, M_HEADS, M_DK), F32)
    zm = jnp.zeros((bp, M_HEADS, LANES), F32)
    gf = row(final_norm)

    p_c, p_n, p_m, p_k, p_v = [], [], [], [], []
    s_c, s_n, s_m, s_k, s_v = [], [], [], [], []
    for i in range(DEPTH):
        j = i // 2
        yp = _ffn(yp, row(ffn_norm1[i]), w_in1[i], w_out1[i], gf, tm=tm_p, final=False)
        ys = _ffn(ys, row(ffn_norm1[i]), w_in1[i], w_out1[i], gf, tm=tm_s, final=False)
        gmix = row(mix_norm[i])
        if i % 2 == 0:
            hnorm = row(mlstm_head_norm[j])
            qkvo, gates = _mproj(yp, gmix, m_w_main[j], m_w_gates[j], m_b_gates[j], tm=tm_p)
            hg, c, n, m = _mlstm(qkvo.reshape(bp, tp, M_MAIN), gates.reshape(bp, tp, LANES),
                                 zc, zn, zm, hnorm, bg=2, L=CHUNK)
            yp = _oproj(yp, hg.reshape(bp * tp, M_V), m_w_out[j], tm=tm_p)
            p_c.append(c); p_n.append(n); p_m.append(m[:, :, 0])

            qkvo, gates = _mproj(ys, gmix, m_w_main[j], m_w_gates[j], m_b_gates[j], tm=tm_s)
            m0 = jnp.broadcast_to(state_mlstm_m[j].astype(F32)[:, :, None], (bs, M_HEADS, LANES))
            hg, c, n, m = _mlstm(qkvo.reshape(bs, ts, M_MAIN), gates.reshape(bs, ts, LANES),
                                 state_mlstm_C[j].astype(F32), state_mlstm_n[j].astype(F32), m0,
                                 hnorm, bg=bs, L=min(CHUNK, ts))
            ys = _oproj(ys, hg.reshape(bs * ts, M_V), m_w_out[j], tm=tm_s)
            s_c.append(c); s_n.append(n); s_m.append(m[:, :, 0])
        else:
            q, kv = _sproj(yp, gmix, s_wq[j], s_wkv[j], cos_p, sin_p, tm=tm_p)
            kv3 = kv.reshape(bp, tp, 2 * A_KV)
            o = _attn_prompt(q.reshape(bp, tp, D_MODEL), kv3, sinks[j])
            yp = _oproj(yp, o.reshape(bp * tp, D_MODEL), s_w_out[j], tm=tm_p)
            keep = min(WINDOW, tp)
            p_k.append(kv3[:, tp - keep:, :A_KV].reshape(bp, keep, A_KV_HEADS, A_HD))
            p_v.append(kv3[:, tp - keep:, A_KV:].reshape(bp, keep, A_KV_HEADS, A_HD))

            q, kv = _sproj(ys, gmix, s_wq[j], s_wkv[j], cos_s, sin_s, tm=tm_s)
            kv3 = kv.reshape(bs, ts, 2 * A_KV)
            rows = cache_swa_k.shape[2]
            o = _attn_sample(q.reshape(bs, ts, D_MODEL),
                             cache_swa_k[j].astype(F32).reshape(bs, rows, A_KV),
                             cache_swa_v[j].astype(F32).reshape(bs, rows, A_KV), kv3, sinks[j])
            ys = _oproj(ys, o.reshape(bs * ts, D_MODEL), s_w_out[j], tm=tm_s)
            s_k.append(kv3[:, :, :A_KV].reshape(bs, ts, A_KV_HEADS, A_HD))
            s_v.append(kv3[:, :, A_KV:].reshape(bs, ts, A_KV_HEADS, A_HD))
        last = i == DEPTH - 1
        yp = _ffn(yp, row(ffn_norm2[i]), w_in2[i], w_out2[i], gf, tm=tm_p, final=last)
        ys = _ffn(ys, row(ffn_norm2[i]), w_in2[i], w_out2[i], gf, tm=tm_s, final=last)

    return (yp.reshape(bp, tp, D_MODEL), ys.reshape(bs, ts, D_MODEL),
            jnp.stack(p_c), jnp.stack(p_n), jnp.stack(p_m), jnp.stack(p_k), jnp.stack(p_v),
            jnp.stack(s_c), jnp.stack(s_n), jnp.stack(s_m), jnp.stack(s_k), jnp.stack(s_v))
```

```python
import functools

import jax
import jax.numpy as jnp
from jax import lax
from jax.experimental import pallas as pl
from jax.experimental.pallas import tpu as pltpu

F32 = jnp.float32
BF16 = jnp.bfloat16

D_MODEL = 1024
DEPTH = 4
CHUNK = 64
M_HEADS = 4
M_DK = D_MODEL // 8
M_DV = D_MODEL // M_HEADS
M_QK = M_HEADS * M_DK
M_V = M_HEADS * M_DV
M_MAIN = 2 * M_QK + 2 * M_V
A_HEADS = 16
A_KV_HEADS = 4
A_HD = D_MODEL // A_HEADS
A_GROUP = A_HEADS // A_KV_HEADS
A_KV = A_KV_HEADS * A_HD
WINDOW = 128
PAST_LEN = 4096
ROPE_THETA = 10000.0
D_FF = 11 * D_MODEL // 4
FFN_RES = 0.5
EPS = 1e-6

LANES = 128
FFN_CHUNK = 256
PROJ_CHUNK = 512
ATTN_QB = 256
M_SCAN_CHUNK = 256
MIB = 1024 * 1024


def _rms(x, g):
    return x * lax.rsqrt(jnp.mean(x * x, axis=-1, keepdims=True) + EPS) * g


def _const_spec(shape):
    nd = len(shape)
    return pl.BlockSpec(shape, lambda *_: (0,) * nd, pipeline_mode=pl.Buffered(1))


def _params(n_grid, vmem_mib):
    return pltpu.CompilerParams(
        dimension_semantics=("arbitrary",) * n_grid,
        vmem_limit_bytes=vmem_mib * MIB)


def _ffn_body(x_ref, g_ref, win_ref, wout_ref, gf_ref, o_ref, h_ref, *, final):
    x = x_ref[...]
    xn = _rms(x, g_ref[...]).astype(BF16)
    for c in range(D_FF // FFN_CHUNK):
        lo = c * FFN_CHUNK
        gate = jnp.dot(xn, win_ref[:, lo:lo + FFN_CHUNK], preferred_element_type=F32)
        up = jnp.dot(xn, win_ref[:, D_FF + lo:D_FF + lo + FFN_CHUNK], preferred_element_type=F32)
        h_ref[:, lo:lo + FFN_CHUNK] = (gate * jax.nn.sigmoid(gate) * up).astype(BF16)
    y = x + FFN_RES * jnp.dot(h_ref[...], wout_ref[...], preferred_element_type=F32)
    if final:
        y = _rms(y, gf_ref[...])
    o_ref[...] = y


def _ffn(x, g, w_in, w_out, g_final, *, tm, final):
    n = x.shape[0]
    return pl.pallas_call(
        functools.partial(_ffn_body, final=final),
        grid=(n // tm,),
        in_specs=[
            pl.BlockSpec((tm, D_MODEL), lambda i: (i, 0)),
            _const_spec((1, D_MODEL)),
            _const_spec((D_MODEL, 2 * D_FF)),
            _const_spec((D_FF, D_MODEL)),
            _const_spec((1, D_MODEL)),
        ],
        out_specs=pl.BlockSpec((tm, D_MODEL), lambda i: (i, 0)),
        out_shape=jax.ShapeDtypeStruct((n, D_MODEL), F32),
        scratch_shapes=[pltpu.VMEM((tm, D_FF), BF16)],
        compiler_params=_params(1, 48),
        name="half_ffn",
    )(x, g, w_in, w_out, g_final)


def _mproj_body(x_ref, g_ref, w_ref, wg_ref, bg_ref, o_ref, og_ref):
    xn = _rms(x_ref[...], g_ref[...]).astype(BF16)
    for c in range(M_MAIN // PROJ_CHUNK):
        lo = c * PROJ_CHUNK
        o_ref[:, lo:lo + PROJ_CHUNK] = jnp.dot(
            xn, w_ref[:, lo:lo + PROJ_CHUNK], preferred_element_type=F32).astype(BF16)
    og_ref[...] = jnp.dot(xn, wg_ref[...], preferred_element_type=F32) + bg_ref[...]


def _mproj(x, g, w, wg, bg, *, tm):
    n = x.shape[0]
    return pl.pallas_call(
        _mproj_body,
        grid=(n // tm,),
        in_specs=[
            pl.BlockSpec((tm, D_MODEL), lambda i: (i, 0)),
            _const_spec((1, D_MODEL)),
            _const_spec((D_MODEL, M_MAIN)),
            _const_spec((D_MODEL, LANES)),
            _const_spec((1, LANES)),
        ],
        out_specs=[
            pl.BlockSpec((tm, M_MAIN), lambda i: (i, 0)),
            pl.BlockSpec((tm, LANES), lambda i: (i, 0)),
        ],
        out_shape=[
            jax.ShapeDtypeStruct((n, M_MAIN), BF16),
            jax.ShapeDtypeStruct((n, LANES), F32),
        ],
        compiler_params=_params(1, 40),
        name="mlstm_in_proj",
    )(x, g, w, wg, bg)


def _mlstm_body(qkvo_ref, gates_ref, c0_ref, n0_ref, m0_ref, hn_ref,
                hg_ref, c_ref, n_ref, m_ref, *, bg, L):
    @pl.when(pl.program_id(1) == 0)
    def _():
        c_ref[...] = c0_ref[...]
        n_ref[...] = n0_ref[...]
        m_ref[...] = m0_ref[...]

    scale = M_DK ** -0.5
    row = lax.broadcasted_iota(jnp.int32, (L, L), 0)
    col = lax.broadcasted_iota(jnp.int32, (L, L), 1)
    causal = col <= row
    tril = causal.astype(BF16)
    lane = lax.broadcasted_iota(jnp.int32, (L, LANES), 1)

    probs = []
    for b in range(bg):
        gts = gates_ref[b]
        lf = jax.nn.log_sigmoid(gts)
        lf_hi = lf.astype(BF16)
        r1 = lf - lf_hi.astype(F32)
        lf_mid = r1.astype(BF16)
        lf_lo = (r1 - lf_mid.astype(F32)).astype(BF16)
        bcs = (jnp.dot(tril, lf_hi, preferred_element_type=F32)
               + jnp.dot(tril, lf_mid, preferred_element_type=F32)
               + jnp.dot(tril, lf_lo, preferred_element_type=F32))
        both = jnp.where(lane < M_HEADS, gts, bcs)
        if L % LANES:
            both = jnp.concatenate([both, jnp.zeros((LANES - L, LANES), F32)], axis=0)
        both_t = both.T
        for h in range(M_HEADS):
            p = dict(b=b, h=h)
            p["ig_row"] = both_t[h:h + 1, 0:L]
            p["b_row"] = both_t[M_HEADS + h:M_HEADS + h + 1, 0:L]
            p["ig_col"] = gts[:, h:h + 1]
            p["b_col"] = bcs[:, M_HEADS + h:M_HEADS + h + 1]
            p["m_prev"] = m_ref[b, h:h + 1, 0:1]
            p["n_row"] = n_ref[b, h:h + 1, :]
            p["c_prev"] = c_ref[b, h]
            p["q"] = qkvo_ref[b, :, h * M_DK:(h + 1) * M_DK]
            p["k"] = qkvo_ref[b, :, M_QK + h * M_DK:M_QK + (h + 1) * M_DK]
            p["v"] = qkvo_ref[b, :, 2 * M_QK + h * M_DV:2 * M_QK + (h + 1) * M_DV]
            probs.append(p)

    for p in probs:
        p["s"] = lax.dot_general(p["q"], p["k"], (((1,), (1,)), ((), ())), preferred_element_type=F32)
        p["qc"] = jnp.dot(p["q"], p["c_prev"].astype(BF16), preferred_element_type=F32)

    for p in probs:
        d = jnp.where(causal, p["b_col"] - p["b_row"] + p["ig_row"], -jnp.inf)
        g_col = p["b_col"] + p["m_prev"]
        m_t = jnp.maximum(g_col, jnp.max(d, axis=-1, keepdims=True))
        w = jnp.exp(d - m_t) * (p["s"] * scale)
        p["m_t"] = m_t
        p["inter"] = jnp.exp(g_col - m_t)
        p["w_sum"] = jnp.sum(w, axis=-1, keepdims=True)
        p["w"] = w.astype(BF16)
        b_last = p["b_row"][:, L - 1:L]
        a_row = b_last - p["b_row"] + p["ig_row"]
        a_col = b_last - p["b_col"] + p["ig_col"]
        m_new = jnp.maximum(p["m_prev"] + b_last, jnp.max(a_row, axis=-1, keepdims=True))
        p["m_new"] = m_new
        p["decay"] = jnp.exp(p["m_prev"] + b_last - m_new)
        p["wk"] = (jnp.exp(a_col - m_new) * scale) * p["k"].astype(F32)

    for p in probs:
        p["wv"] = jnp.dot(p["w"], p["v"], preferred_element_type=F32)
        p["kv"] = lax.dot_general(p["wk"].astype(BF16), p["v"], (((0,), (0,)), ((), ())),
                                  preferred_element_type=F32)

    for p in probs:
        b, h = p["b"], p["h"]
        num = p["inter"] * p["qc"] + p["wv"]
        qn = jnp.sum(p["q"].astype(F32) * p["n_row"], axis=-1, keepdims=True)
        den = p["inter"] * qn + p["w_sum"]
        hh = num / jnp.maximum(jnp.abs(den), jnp.exp(-p["m_t"]))
        hh = hh * lax.rsqrt(jnp.mean(hh * hh, axis=-1, keepdims=True) + EPS)
        hh = hh * hn_ref[:, h * M_DV:(h + 1) * M_DV]
        og = qkvo_ref[b, :, 2 * M_QK + M_V + h * M_DV:2 * M_QK + M_V + (h + 1) * M_DV]
        hg_ref[b, :, h * M_DV:(h + 1) * M_DV] = (jax.nn.sigmoid(og.astype(F32)) * hh).astype(BF16)
        c_ref[b, h] = p["decay"] * p["c_prev"] + p["kv"]
        n_ref[b, h:h + 1, :] = p["decay"] * p["n_row"] + jnp.sum(p["wk"], axis=0, keepdims=True)
        m_ref[b, h:h + 1, :] = jnp.broadcast_to(p["m_new"], (1, LANES))


def _mlstm(qkvo, gates, c0, n0, m0, hnorm, *, bg, L):
    nb, t, _ = qkvo.shape
    state_specs = [
        pl.BlockSpec((bg, M_HEADS, M_DK, M_DV), lambda g, c: (g, 0, 0, 0)),
        pl.BlockSpec((bg, M_HEADS, M_DK), lambda g, c: (g, 0, 0)),
        pl.BlockSpec((bg, M_HEADS, LANES), lambda g, c: (g, 0, 0)),
    ]
    return pl.pallas_call(
        functools.partial(_mlstm_body, bg=bg, L=L),
        grid=(nb // bg, t // L),
        in_specs=[
            pl.BlockSpec((bg, L, M_MAIN), lambda g, c: (g, c, 0)),
            pl.BlockSpec((bg, L, LANES), lambda g, c: (g, c, 0)),
            *state_specs,
            _const_spec((1, M_V)),
        ],
        out_specs=[pl.BlockSpec((bg, L, M_V), lambda g, c: (g, c, 0)), *state_specs],
        out_shape=[
            jax.ShapeDtypeStruct((nb, t, M_V), BF16),
            jax.ShapeDtypeStruct((nb, M_HEADS, M_DK, M_DV), F32),
            jax.ShapeDtypeStruct((nb, M_HEADS, M_DK), F32),
            jax.ShapeDtypeStruct((nb, M_HEADS, LANES), F32),
        ],
        compiler_params=_params(2, 32),
        name="mlstm_scan",
    )(qkvo, gates, c0, n0, m0, hnorm)


def _rope(x, cos, sin_signed, first_half):
    swapped = jnp.where(first_half, pltpu.roll(x, LANES - A_HD // 2, 1), pltpu.roll(x, A_HD // 2, 1))
    return x * cos + swapped * sin_signed


def _sproj_body(x_ref, g_ref, wq_ref, wkv_ref, cos_ref, sin_ref, q_ref, kv_ref):
    xn = _rms(x_ref[...], g_ref[...]).astype(BF16)
    cos = cos_ref[...]
    sin_signed = sin_ref[...]
    lane = lax.broadcasted_iota(jnp.int32, cos.shape, 1)
    first_half = (lane & (A_HD // 2)) == 0
    q_scale = A_HD ** -0.5
    for c in range(D_MODEL // PROJ_CHUNK):
        q = jnp.dot(xn, wq_ref[:, c * PROJ_CHUNK:(c + 1) * PROJ_CHUNK], preferred_element_type=F32)
        for j in range(PROJ_CHUNK // LANES):
            lo = c * PROJ_CHUNK + j * LANES
            blk = _rope(q[:, j * LANES:(j + 1) * LANES], cos, sin_signed, first_half)
            q_ref[:, lo:lo + LANES] = (blk * q_scale).astype(BF16)
    kv = jnp.dot(xn, wkv_ref[...], preferred_element_type=F32)
    for j in range(A_KV // LANES):
        kv_ref[:, j * LANES:(j + 1) * LANES] = _rope(
            kv[:, j * LANES:(j + 1) * LANES], cos, sin_signed, first_half)
    kv_ref[:, A_KV:2 * A_KV] = kv[:, A_KV:2 * A_KV]


def _sproj(x, g, wq, wkv, cos, sin_signed, *, tm):
    n = x.shape[0]
    n_tab = cos.shape[0] // tm
    return pl.pallas_call(
        _sproj_body,
        grid=(n // tm,),
        in_specs=[
            pl.BlockSpec((tm, D_MODEL), lambda i: (i, 0)),
            _const_spec((1, D_MODEL)),
            _const_spec((D_MODEL, D_MODEL)),
            _const_spec((D_MODEL, 2 * A_KV)),
            pl.BlockSpec((tm, LANES), lambda i: (i % n_tab, 0)),
            pl.BlockSpec((tm, LANES), lambda i: (i % n_tab, 0)),
        ],
        out_specs=[
            pl.BlockSpec((tm, D_MODEL), lambda i: (i, 0)),
            pl.BlockSpec((tm, 2 * A_KV), lambda i: (i, 0)),
        ],
        out_shape=[
            jax.ShapeDtypeStruct((n, D_MODEL), BF16),
            jax.ShapeDtypeStruct((n, 2 * A_KV), F32),
        ],
        compiler_params=_params(1, 32),
        name="swa_in_proj",
    )(x, g, wq, wkv, cos, sin_signed)


def _attn_groups(sinks_ref, q_ref, o_ref, kv_scr, problems):
    scores = []
    for q_lo, q_rows, k_lo, k_rows, g, first_valid in problems:
        kb = kv_scr[k_lo:k_lo + k_rows, g * A_HD:(g + 1) * A_HD]
        qs = jnp.concatenate(
            [q_ref[0, q_lo:q_lo + q_rows, hd * A_HD:(hd + 1) * A_HD]
             for hd in range(A_GROUP * g, A_GROUP * (g + 1))], axis=0)
        scores.append(lax.dot_general(kb, qs, (((1,), (1,)), ((), ())), preferred_element_type=F32))
    probs = []
    for s, (q_lo, q_rows, k_lo, k_rows, g, first_valid) in zip(scores, problems):
        if first_valid is not None:
            key = lax.broadcasted_iota(jnp.int32, s.shape, 0)
            s = jnp.where(key >= first_valid, s, -jnp.inf)
        sk = jnp.concatenate([jnp.full((1, q_rows), sinks_ref[hd], F32)
                              for hd in range(A_GROUP * g, A_GROUP * (g + 1))], axis=1)
        m = jnp.maximum(jnp.max(s, axis=0, keepdims=True), sk)
        p = jnp.exp(s - m)
        den = jnp.sum(p, axis=0, keepdims=True) + jnp.exp(sk - m)
        probs.append((p.astype(BF16), den))
    for (p, den), (q_lo, q_rows, k_lo, k_rows, g, first_valid) in zip(probs, problems):
        vb = kv_scr[k_lo:k_lo + k_rows, A_KV + g * A_HD:A_KV + (g + 1) * A_HD]
        o_t = lax.dot_general(vb, p, (((0,), (0,)), ((), ())), preferred_element_type=F32) / den
        o = o_t.T
        for j in range(A_GROUP):
            hd = A_GROUP * g + j
            o_ref[0, q_lo:q_lo + q_rows, hd * A_HD:(hd + 1) * A_HD] = (
                o[j * q_rows:(j + 1) * q_rows].astype(BF16))


def _attn_prompt_body(sinks_ref, q_ref, kvp_ref, kvo_ref, o_ref, kv_scr):
    kv_scr[0:WINDOW, :] = kvp_ref[0].astype(BF16)
    kv_scr[WINDOW:WINDOW + ATTN_QB, :] = kvo_ref[0].astype(BF16)
    band = WINDOW + CHUNK
    n_missing = jnp.where(pl.program_id(1) == 0, WINDOW, 0)
    problems = []
    for i in range(ATTN_QB // CHUNK):
        first_valid = (n_missing - i * CHUNK) if i * CHUNK < WINDOW else None
        for g in range(A_KV_HEADS):
            problems.append((i * CHUNK, CHUNK, i * CHUNK, band, g, first_valid))
    _attn_groups(sinks_ref, q_ref, o_ref, kv_scr, problems)


def _attn_prompt(q, kv, sinks):
    nb, t, _ = q.shape
    per_prev = ATTN_QB // WINDOW
    return pl.pallas_call(
        _attn_prompt_body,
        grid=(nb, t // ATTN_QB),
        in_specs=[
            pl.BlockSpec(memory_space=pltpu.SMEM),
            pl.BlockSpec((1, ATTN_QB, D_MODEL), lambda b, i: (b, i, 0)),
            pl.BlockSpec((1, WINDOW, 2 * A_KV), lambda b, i: (b, jnp.maximum(i * per_prev - 1, 0), 0)),
            pl.BlockSpec((1, ATTN_QB, 2 * A_KV), lambda b, i: (b, i, 0)),
        ],
        out_specs=pl.BlockSpec((1, ATTN_QB, D_MODEL), lambda b, i: (b, i, 0)),
        out_shape=jax.ShapeDtypeStruct((nb, t, D_MODEL), BF16),
        scratch_shapes=[pltpu.VMEM((WINDOW + ATTN_QB, 2 * A_KV), BF16)],
        compiler_params=_params(2, 32),
        name="swa_prompt_attn",
    )(sinks, q, kv, kv)


def _attn_sample_body(sinks_ref, q_ref, ck_ref, cv_ref, kv_ref, o_ref, kv_scr):
    rows = ck_ref.shape[1]
    t = q_ref.shape[1]
    kv_scr[0:rows, 0:A_KV] = ck_ref[0].astype(BF16)
    kv_scr[0:rows, A_KV:2 * A_KV] = cv_ref[0].astype(BF16)
    kv_scr[rows:rows + t, :] = kv_ref[0].astype(BF16)
    _attn_groups(sinks_ref, q_ref, o_ref, kv_scr,
                 [(0, t, 0, rows + t, g, None) for g in range(A_KV_HEADS)])


def _attn_sample(q, cache_k, cache_v, kv, sinks):
    nb, t, _ = q.shape
    rows = cache_k.shape[1]
    return pl.pallas_call(
        _attn_sample_body,
        grid=(nb,),
        in_specs=[
            pl.BlockSpec(memory_space=pltpu.SMEM),
            pl.BlockSpec((1, t, D_MODEL), lambda b: (b, 0, 0)),
            pl.BlockSpec((1, rows, A_KV), lambda b: (b, 0, 0)),
            pl.BlockSpec((1, rows, A_KV), lambda b: (b, 0, 0)),
            pl.BlockSpec((1, t, 2 * A_KV), lambda b: (b, 0, 0)),
        ],
        out_specs=pl.BlockSpec((1, t, D_MODEL), lambda b: (b, 0, 0)),
        out_shape=jax.ShapeDtypeStruct((nb, t, D_MODEL), BF16),
        scratch_shapes=[pltpu.VMEM((rows + t, 2 * A_KV), BF16)],
        compiler_params=_params(1, 32),
        name="swa_sample_attn",
    )(sinks, q, cache_k, cache_v, kv)


def _oproj_body(x_ref, a_ref, w_ref, o_ref):
    o_ref[...] = x_ref[...] + jnp.dot(a_ref[...], w_ref[...], preferred_element_type=F32)


def _oproj(x, a, w, *, tm):
    n = x.shape[0]
    return pl.pallas_call(
        _oproj_body,
        grid=(n // tm,),
        in_specs=[
            pl.BlockSpec((tm, D_MODEL), lambda i: (i, 0)),
            pl.BlockSpec((tm, D_MODEL), lambda i: (i, 0)),
            _const_spec((D_MODEL, D_MODEL)),
        ],
        out_specs=pl.BlockSpec((tm, D_MODEL), lambda i: (i, 0)),
        out_shape=jax.ShapeDtypeStruct((n, D_MODEL), F32),
        compiler_params=_params(1, 32),
        name="mixer_out_proj",
    )(x, a, w)


def _rope_tables(pos):
    inv = ROPE_THETA ** (-jnp.arange(0, A_HD, 2, dtype=F32) / A_HD)
    ang = pos.astype(F32)[:, None] * inv[None, :]
    cos = jnp.cos(ang)
    sin = jnp.sin(ang)
    reps = LANES // A_HD
    return (jnp.concatenate([cos, cos] * reps, axis=-1),
            jnp.concatenate([-sin, sin] * reps, axis=-1))


def kernel(x_prompt, x_sample, state_mlstm_C, state_mlstm_n, state_mlstm_m, cache_swa_k, cache_swa_v,
           ffn_norm1, ffn_w_in1, ffn_w_out1, mix_norm, mlstm_w_in, mlstm_b_gates, mlstm_head_norm,
           mlstm_w_out, swa_w_qkv, swa_sinks, swa_w_out, ffn_norm2, ffn_w_in2, ffn_w_out2, final_norm):
    bp, tp, _ = x_prompt.shape
    bs, ts, _ = x_sample.shape
    tm_p = 512
    tm_s = bs * ts

    w_in1, w_out1 = ffn_w_in1.astype(BF16), ffn_w_out1.astype(BF16)
    w_in2, w_out2 = ffn_w_in2.astype(BF16), ffn_w_out2.astype(BF16)
    m_w_main = mlstm_w_in[:, :, :M_MAIN].astype(BF16)
    n_gates = 2 * M_HEADS
    m_w_gates = jnp.pad(mlstm_w_in[:, :, M_MAIN:], ((0, 0), (0, 0), (0, LANES - n_gates))).astype(BF16)
    m_b_gates = jnp.pad(mlstm_b_gates.astype(F32), ((0, 0), (0, LANES - n_gates)))[:, None, :]
    m_w_out = mlstm_w_out.astype(BF16)
    s_wq = swa_w_qkv[:, :, :D_MODEL].astype(BF16)
    s_wkv = swa_w_qkv[:, :, D_MODEL:].astype(BF16)
    s_w_out = swa_w_out.astype(BF16)
    sinks = swa_sinks.astype(F32)

    cos_p, sin_p = _rope_tables(jnp.arange(tp))
    cos_s, sin_s = _rope_tables(PAST_LEN + jnp.arange(ts))
    cos_s, sin_s = jnp.tile(cos_s, (bs, 1)), jnp.tile(sin_s, (bs, 1))

    row = lambda a: a.astype(F32)[None, :]
    yp = x_prompt.reshape(bp * tp, D_MODEL)
    ys = x_sample.reshape(bs * ts, D_MODEL)
    zc = jnp.zeros((bp, M_HEADS, M_DK, M_DV), F32)
    zn = jnp.zeros((bp, M_HEADS, M_DK), F32)
    zm = jnp.zeros((bp, M_HEADS, LANES), F32)
    gf = row(final_norm)

    p_c, p_n, p_m, p_k, p_v = [], [], [], [], []
    s_c, s_n, s_m, s_k, s_v = [], [], [], [], []
    for i in range(DEPTH):
        j = i // 2
        yp = _ffn(yp, row(ffn_norm1[i]), w_in1[i], w_out1[i], gf, tm=tm_p, final=False)
        ys = _ffn(ys, row(ffn_norm1[i]), w_in1[i], w_out1[i], gf, tm=tm_s, final=False)
        gmix = row(mix_norm[i])
        if i % 2 == 0:
            hnorm = row(mlstm_head_norm[j])
            qkvo, gates = _mproj(yp, gmix, m_w_main[j], m_w_gates[j], m_b_gates[j], tm=tm_p)
            hg, c, n, m = _mlstm(qkvo.reshape(bp, tp, M_MAIN), gates.reshape(bp, tp, LANES),
                                 zc, zn, zm, hnorm, bg=2, L=min(M_SCAN_CHUNK, tp))
            yp = _oproj(yp, hg.reshape(bp * tp, M_V), m_w_out[j], tm=tm_p)
            p_c.append(c); p_n.append(n); p_m.append(m[:, :, 0])

            qkvo, gates = _mproj(ys, gmix, m_w_main[j], m_w_gates[j], m_b_gates[j], tm=tm_s)
            m0 = jnp.broadcast_to(state_mlstm_m[j].astype(F32)[:, :, None], (bs, M_HEADS, LANES))
            hg, c, n, m = _mlstm(qkvo.reshape(bs, ts, M_MAIN), gates.reshape(bs, ts, LANES),
                                 state_mlstm_C[j].astype(F32), state_mlstm_n[j].astype(F32), m0,
                                 hnorm, bg=bs, L=min(CHUNK, ts))
            ys = _oproj(ys, hg.reshape(bs * ts, M_V), m_w_out[j], tm=tm_s)
            s_c.append(c); s_n.append(n); s_m.append(m[:, :, 0])
        else:
            q, kv = _sproj(yp, gmix, s_wq[j], s_wkv[j], cos_p, sin_p, tm=tm_p)
            kv3 = kv.reshape(bp, tp, 2 * A_KV)
            o = _attn_prompt(q.reshape(bp, tp, D_MODEL), kv3, sinks[j])
            yp = _oproj(yp, o.reshape(bp * tp, D_MODEL), s_w_out[j], tm=tm_p)
            keep = min(WINDOW, tp)
            p_k.append(kv3[:, tp - keep:, :A_KV].reshape(bp, keep, A_KV_HEADS, A_HD))
            p_v.append(kv3[:, tp - keep:, A_KV:].reshape(bp, keep, A_KV_HEADS, A_HD))

            q, kv = _sproj(ys, gmix, s_wq[j], s_wkv[j], cos_s, sin_s, tm=tm_s)
            kv3 = kv.reshape(bs, ts, 2 * A_KV)
            rows = cache_swa_k.shape[2]
            o = _attn_sample(q.reshape(bs, ts, D_MODEL),
                             cache_swa_k[j].astype(F32).reshape(bs, rows, A_KV),
                             cache_swa_v[j].astype(F32).reshape(bs, rows, A_KV), kv3, sinks[j])
            ys = _oproj(ys, o.reshape(bs * ts, D_MODEL), s_w_out[j], tm=tm_s)
            s_k.append(kv3[:, :, :A_KV].reshape(bs, ts, A_KV_HEADS, A_HD))
            s_v.append(kv3[:, :, A_KV:].reshape(bs, ts, A_KV_HEADS, A_HD))
        last = i == DEPTH - 1
        yp = _ffn(yp, row(ffn_norm2[i]), w_in2[i], w_out2[i], gf, tm=tm_p, final=last)
        ys = _ffn(ys, row(ffn_norm2[i]), w_in2[i], w_out2[i], gf, tm=tm_s, final=last)

    return (yp.reshape(bp, tp, D_MODEL), ys.reshape(bs, ts, D_MODEL),
            jnp.stack(p_c), jnp.stack(p_n), jnp.stack(p_m), jnp.stack(p_k), jnp.stack(p_v),
            jnp.stack(s_c), jnp.stack(s_n), jnp.stack(s_m), jnp.stack(s_k), jnp.stack(s_v))
```

```python
import functools

import jax
import jax.numpy as jnp
from jax import lax
from jax.experimental import pallas as pl
from jax.experimental.pallas import tpu as pltpu

F32 = jnp.float32
BF16 = jnp.bfloat16

D_MODEL = 1024
DEPTH = 4
CHUNK = 64
M_HEADS = 4
M_DK = D_MODEL // 8
M_DV = D_MODEL // M_HEADS
M_QK = M_HEADS * M_DK
M_V = M_HEADS * M_DV
M_MAIN = 2 * M_QK + 2 * M_V
A_HEADS = 16
A_KV_HEADS = 4
A_HD = D_MODEL // A_HEADS
A_GROUP = A_HEADS // A_KV_HEADS
A_KV = A_KV_HEADS * A_HD
WINDOW = 128
PAST_LEN = 4096
ROPE_THETA = 10000.0
D_FF = 11 * D_MODEL // 4
FFN_RES = 0.5
EPS = 1e-6

LANES = 128
FFN_CHUNK = 256
PROJ_CHUNK = 512
ATTN_QB = 256
M_SCAN_CHUNK = 256
MIB = 1024 * 1024


def _rms(x, g):
    return x * lax.rsqrt(jnp.mean(x * x, axis=-1, keepdims=True) + EPS) * g


def _const_spec(shape):
    nd = len(shape)
    return pl.BlockSpec(shape, lambda *_: (0,) * nd, pipeline_mode=pl.Buffered(1))


def _params(n_grid, vmem_mib):
    return pltpu.CompilerParams(
        dimension_semantics=("arbitrary",) * n_grid,
        vmem_limit_bytes=vmem_mib * MIB)


def _ffn_body(*refs, mixer_out, final):
    if mixer_out:
        x_ref, a_ref, wmix_ref, g_ref, win_ref, wout_ref, gf_ref, o_ref, h_ref = refs
        x = x_ref[...] + jnp.dot(a_ref[...], wmix_ref[...], preferred_element_type=F32)
    else:
        x_ref, g_ref, win_ref, wout_ref, gf_ref, o_ref, h_ref = refs
        x = x_ref[...]
    xn = _rms(x, g_ref[...]).astype(BF16)
    for c in range(D_FF // FFN_CHUNK):
        lo = c * FFN_CHUNK
        gate = jnp.dot(xn, win_ref[:, lo:lo + FFN_CHUNK], preferred_element_type=F32)
        up = jnp.dot(xn, win_ref[:, D_FF + lo:D_FF + lo + FFN_CHUNK], preferred_element_type=F32)
        h_ref[:, lo:lo + FFN_CHUNK] = (gate * jax.nn.sigmoid(gate) * up).astype(BF16)
    y = x + FFN_RES * jnp.dot(h_ref[...], wout_ref[...], preferred_element_type=F32)
    if final:
        y = _rms(y, gf_ref[...])
    o_ref[...] = y


def _ffn(x, g, w_in, w_out, g_final, *, tm, final, mixer_out=None):
    n = x.shape[0]
    tile = pl.BlockSpec((tm, D_MODEL), lambda i: (i, 0))
    mix_specs = [tile, _const_spec((D_MODEL, D_MODEL))] if mixer_out else []
    return pl.pallas_call(
        functools.partial(_ffn_body, mixer_out=bool(mixer_out), final=final),
        grid=(n // tm,),
        in_specs=[
            tile,
            *mix_specs,
            _const_spec((1, D_MODEL)),
            _const_spec((D_MODEL, 2 * D_FF)),
            _const_spec((D_FF, D_MODEL)),
            _const_spec((1, D_MODEL)),
        ],
        out_specs=tile,
        out_shape=jax.ShapeDtypeStruct((n, D_MODEL), F32),
        scratch_shapes=[pltpu.VMEM((tm, D_FF), BF16)],
        compiler_params=_params(1, 52),
        name="half_ffn",
    )(x, *(mixer_out or ()), g, w_in, w_out, g_final)


def _mproj_body(x_ref, g_ref, w_ref, wg_ref, bg_ref, o_ref, og_ref):
    xn = _rms(x_ref[...], g_ref[...]).astype(BF16)
    for c in range(M_MAIN // PROJ_CHUNK):
        lo = c * PROJ_CHUNK
        o_ref[:, lo:lo + PROJ_CHUNK] = jnp.dot(
            xn, w_ref[:, lo:lo + PROJ_CHUNK], preferred_element_type=F32).astype(BF16)
    og_ref[...] = jnp.dot(xn, wg_ref[...], preferred_element_type=F32) + bg_ref[...]


def _mproj(x, g, w, wg, bg, *, tm):
    n = x.shape[0]
    return pl.pallas_call(
        _mproj_body,
        grid=(n // tm,),
        in_specs=[
            pl.BlockSpec((tm, D_MODEL), lambda i: (i, 0)),
            _const_spec((1, D_MODEL)),
            _const_spec((D_MODEL, M_MAIN)),
            _const_spec((D_MODEL, LANES)),
            _const_spec((1, LANES)),
        ],
        out_specs=[
            pl.BlockSpec((tm, M_MAIN), lambda i: (i, 0)),
            pl.BlockSpec((tm, LANES), lambda i: (i, 0)),
        ],
        out_shape=[
            jax.ShapeDtypeStruct((n, M_MAIN), BF16),
            jax.ShapeDtypeStruct((n, LANES), F32),
        ],
        compiler_params=_params(1, 40),
        name="mlstm_in_proj",
    )(x, g, w, wg, bg)


def _mlstm_body(qkvo_ref, gates_ref, c0_ref, n0_ref, m0_ref, hn_ref,
                hg_ref, c_ref, n_ref, m_ref, *, bg, L):
    @pl.when(pl.program_id(1) == 0)
    def _():
        c_ref[...] = c0_ref[...]
        n_ref[...] = n0_ref[...]
        m_ref[...] = m0_ref[...]

    scale = M_DK ** -0.5
    row = lax.broadcasted_iota(jnp.int32, (L, L), 0)
    col = lax.broadcasted_iota(jnp.int32, (L, L), 1)
    causal = col <= row
    tril = causal.astype(BF16)
    lane = lax.broadcasted_iota(jnp.int32, (L, LANES), 1)

    probs = []
    for b in range(bg):
        gts = gates_ref[b]
        lf = jax.nn.log_sigmoid(gts)
        lf_hi = lf.astype(BF16)
        r1 = lf - lf_hi.astype(F32)
        lf_mid = r1.astype(BF16)
        lf_lo = (r1 - lf_mid.astype(F32)).astype(BF16)
        bcs = (jnp.dot(tril, lf_hi, preferred_element_type=F32)
               + jnp.dot(tril, lf_mid, preferred_element_type=F32)
               + jnp.dot(tril, lf_lo, preferred_element_type=F32))
        both = jnp.where(lane < M_HEADS, gts, bcs)
        if L % LANES:
            both = jnp.concatenate([both, jnp.zeros((LANES - L, LANES), F32)], axis=0)
        both_t = both.T
        for h in range(M_HEADS):
            p = dict(b=b, h=h)
            p["ig_row"] = both_t[h:h + 1, 0:L]
            p["b_row"] = both_t[M_HEADS + h:M_HEADS + h + 1, 0:L]
            b_col = bcs[:, M_HEADS + h:M_HEADS + h + 1]
            p["b_rep"] = jnp.broadcast_to(b_col, (L, LANES))
            p["c_rep"] = jnp.broadcast_to(gts[:, h:h + 1] - b_col, (L, LANES))
            p["m_prev"] = m_ref[b, h:h + 1, 0:1]
            p["n_row"] = n_ref[b, h:h + 1, :]
            p["c_prev"] = c_ref[b, h]
            p["q"] = qkvo_ref[b, :, h * M_DK:(h + 1) * M_DK]
            p["k"] = qkvo_ref[b, :, M_QK + h * M_DK:M_QK + (h + 1) * M_DK]
            p["v"] = qkvo_ref[b, :, 2 * M_QK + h * M_DV:2 * M_QK + (h + 1) * M_DV]
            probs.append(p)

    def wide(c, width):
        return c[:, :width] if width <= LANES else jnp.concatenate([c] * (width // LANES), axis=1)

    def fold(x):
        acc = x[:, :LANES]
        for t in range(1, x.shape[1] // LANES):
            acc = acc + x[:, t * LANES:(t + 1) * LANES]
        return acc

    for p in probs:
        p["s"] = lax.dot_general(p["q"], p["k"], (((1,), (1,)), ((), ())), preferred_element_type=F32)
        p["qc"] = jnp.dot(p["q"], p["c_prev"].astype(BF16), preferred_element_type=F32)
        n_rep = jnp.broadcast_to(p["n_row"], (LANES, M_DK)).astype(BF16)
        p["qn"] = lax.dot_general(p["q"], n_rep, (((1,), (1,)), ((), ())), preferred_element_type=F32)

    for p in probs:
        d = jnp.where(causal, wide(p["b_rep"], L) - p["b_row"] + p["ig_row"], -jnp.inf)
        g_rep = p["b_rep"] + p["m_prev"]
        m_t = jnp.maximum(g_rep, jnp.max(d, axis=-1, keepdims=True))
        w = jnp.exp(d - wide(m_t, L)) * (p["s"] * scale)
        p["m_t"] = m_t
        p["inter"] = jnp.exp(g_rep - m_t)
        p["w_sum"] = jnp.sum(fold(w), axis=-1, keepdims=True)
        p["w"] = w.astype(BF16)
        b_last = p["b_row"][:, L - 1:L]
        a_row = b_last - p["b_row"] + p["ig_row"]
        m_new = jnp.maximum(p["m_prev"] + b_last, jnp.max(a_row, axis=-1, keepdims=True))
        p["m_new"] = m_new
        p["decay"] = jnp.exp(p["m_prev"] + b_last - m_new)
        p["wk"] = (jnp.exp(b_last + p["c_rep"] - m_new) * scale) * p["k"].astype(F32)

    for p in probs:
        p["wv"] = jnp.dot(p["w"], p["v"], preferred_element_type=F32)
        p["kv"] = lax.dot_general(p["wk"].astype(BF16), p["v"], (((0,), (0,)), ((), ())),
                                  preferred_element_type=F32)

    for p in probs:
        b, h = p["b"], p["h"]
        num = wide(p["inter"], M_DV) * p["qc"] + p["wv"]
        den = p["inter"] * p["qn"] + p["w_sum"]
        hh = num / wide(jnp.maximum(jnp.abs(den), jnp.exp(-p["m_t"])), M_DV)
        hh = hh * lax.rsqrt(jnp.mean(hh * hh, axis=-1, keepdims=True) + EPS)
        hh = hh * hn_ref[:, h * M_DV:(h + 1) * M_DV]
        og = qkvo_ref[b, :, 2 * M_QK + M_V + h * M_DV:2 * M_QK + M_V + (h + 1) * M_DV]
        hg_ref[b, :, h * M_DV:(h + 1) * M_DV] = (jax.nn.sigmoid(og.astype(F32)) * hh).astype(BF16)
        c_ref[b, h] = p["decay"] * p["c_prev"] + p["kv"]
        n_ref[b, h:h + 1, :] = p["decay"] * p["n_row"] + jnp.sum(p["wk"], axis=0, keepdims=True)
        m_ref[b, h:h + 1, :] = jnp.broadcast_to(p["m_new"], (1, LANES))


def _mlstm(qkvo, gates, c0, n0, m0, hnorm, *, bg, L):
    nb, t, _ = qkvo.shape
    state_specs = [
        pl.BlockSpec((bg, M_HEADS, M_DK, M_DV), lambda g, c: (g, 0, 0, 0)),
        pl.BlockSpec((bg, M_HEADS, M_DK), lambda g, c: (g, 0, 0)),
        pl.BlockSpec((bg, M_HEADS, LANES), lambda g, c: (g, 0, 0)),
    ]
    return pl.pallas_call(
        functools.partial(_mlstm_body, bg=bg, L=L),
        grid=(nb // bg, t // L),
        in_specs=[
            pl.BlockSpec((bg, L, M_MAIN), lambda g, c: (g, c, 0)),
            pl.BlockSpec((bg, L, LANES), lambda g, c: (g, c, 0)),
            *state_specs,
            _const_spec((1, M_V)),
        ],
        out_specs=[pl.BlockSpec((bg, L, M_V), lambda g, c: (g, c, 0)), *state_specs],
        out_shape=[
            jax.ShapeDtypeStruct((nb, t, M_V), BF16),
            jax.ShapeDtypeStruct((nb, M_HEADS, M_DK, M_DV), F32),
            jax.ShapeDtypeStruct((nb, M_HEADS, M_DK), F32),
            jax.ShapeDtypeStruct((nb, M_HEADS, LANES), F32),
        ],
        compiler_params=_params(2, 32),
        name="mlstm_scan",
    )(qkvo, gates, c0, n0, m0, hnorm)


def _rope(x, cos, sin_signed, first_half):
    swapped = jnp.where(first_half, pltpu.roll(x, LANES - A_HD // 2, 1), pltpu.roll(x, A_HD // 2, 1))
    return x * cos + swapped * sin_signed


def _sproj_body(x_ref, g_ref, wq_ref, wkv_ref, cos_ref, sin_ref, q_ref, kv_ref):
    xn = _rms(x_ref[...], g_ref[...]).astype(BF16)
    cos = cos_ref[...]
    sin_signed = sin_ref[...]
    lane = lax.broadcasted_iota(jnp.int32, cos.shape, 1)
    first_half = (lane & (A_HD // 2)) == 0
    q_scale = A_HD ** -0.5
    for c in range(D_MODEL // PROJ_CHUNK):
        q = jnp.dot(xn, wq_ref[:, c * PROJ_CHUNK:(c + 1) * PROJ_CHUNK], preferred_element_type=F32)
        for j in range(PROJ_CHUNK // LANES):
            lo = c * PROJ_CHUNK + j * LANES
            blk = _rope(q[:, j * LANES:(j + 1) * LANES], cos, sin_signed, first_half)
            q_ref[:, lo:lo + LANES] = (blk * q_scale).astype(BF16)
    kv = jnp.dot(xn, wkv_ref[...], preferred_element_type=F32)
    for j in range(A_KV // LANES):
        kv_ref[:, j * LANES:(j + 1) * LANES] = _rope(
            kv[:, j * LANES:(j + 1) * LANES], cos, sin_signed, first_half)
    kv_ref[:, A_KV:2 * A_KV] = kv[:, A_KV:2 * A_KV]


def _sproj(x, g, wq, wkv, cos, sin_signed, *, tm):
    n = x.shape[0]
    n_tab = cos.shape[0] // tm
    return pl.pallas_call(
        _sproj_body,
        grid=(n // tm,),
        in_specs=[
            pl.BlockSpec((tm, D_MODEL), lambda i: (i, 0)),
            _const_spec((1, D_MODEL)),
            _const_spec((D_MODEL, D_MODEL)),
            _const_spec((D_MODEL, 2 * A_KV)),
            pl.BlockSpec((tm, LANES), lambda i: (i % n_tab, 0)),
            pl.BlockSpec((tm, LANES), lambda i: (i % n_tab, 0)),
        ],
        out_specs=[
            pl.BlockSpec((tm, D_MODEL), lambda i: (i, 0)),
            pl.BlockSpec((tm, 2 * A_KV), lambda i: (i, 0)),
        ],
        out_shape=[
            jax.ShapeDtypeStruct((n, D_MODEL), BF16),
            jax.ShapeDtypeStruct((n, 2 * A_KV), F32),
        ],
        compiler_params=_params(1, 32),
        name="swa_in_proj",
    )(x, g, wq, wkv, cos, sin_signed)


def _attn_groups(sinks_ref, q_ref, o_ref, kv_scr, vt_scr, problems):
    scores = []
    for q_lo, q_rows, k_lo, k_rows, g, first_valid in problems:
        kb = kv_scr[k_lo:k_lo + k_rows, g * A_HD:(g + 1) * A_HD]
        qs = jnp.concatenate(
            [q_ref[0, q_lo:q_lo + q_rows, hd * A_HD:(hd + 1) * A_HD]
             for hd in range(A_GROUP * g, A_GROUP * (g + 1))], axis=0)
        scores.append(lax.dot_general(kb, qs, (((1,), (1,)), ((), ())), preferred_element_type=F32))
    probs = []
    for s, (q_lo, q_rows, k_lo, k_rows, g, first_valid) in zip(scores, problems):
        if first_valid is not None:
            key = lax.broadcasted_iota(jnp.int32, s.shape, 0)
            s = jnp.where(key >= first_valid, s, -jnp.inf)
        sk = jnp.concatenate([jnp.full((1, q_rows), sinks_ref[hd], F32)
                              for hd in range(A_GROUP * g, A_GROUP * (g + 1))], axis=1)
        m = jnp.maximum(jnp.max(s, axis=0, keepdims=True), sk)
        p = jnp.exp(s - m)
        den = jnp.sum(p, axis=0, keepdims=True) + jnp.exp(sk - m)
        probs.append((p.astype(BF16), den))
    outs = []
    for (p, den), (q_lo, q_rows, k_lo, k_rows, g, first_valid) in zip(probs, problems):
        if vt_scr is None:
            vb = kv_scr[k_lo:k_lo + k_rows, A_KV + g * A_HD:A_KV + (g + 1) * A_HD]
            o_t = lax.dot_general(vb, p, (((0,), (0,)), ((), ())), preferred_element_type=F32)
        else:
            o_t = jnp.dot(vt_scr[g * A_HD:(g + 1) * A_HD, k_lo:k_lo + k_rows], p,
                          preferred_element_type=F32)
        outs.append(o_t / den)
    for o_t, (q_lo, q_rows, k_lo, k_rows, g, first_valid) in zip(outs, problems):
        o = o_t.T
        for j in range(A_GROUP):
            hd = A_GROUP * g + j
            o_ref[0, q_lo:q_lo + q_rows, hd * A_HD:(hd + 1) * A_HD] = (
                o[j * q_rows:(j + 1) * q_rows].astype(BF16))


def _attn_prompt_body(sinks_ref, q_ref, kvp_ref, kvo_ref, o_ref, kv_scr, vt_scr):
    kv_scr[0:WINDOW, :] = kvp_ref[0].astype(BF16)
    kv_scr[WINDOW:WINDOW + ATTN_QB, :] = kvo_ref[0].astype(BF16)
    vt_scr[:, 0:WINDOW] = kvp_ref[0, :, A_KV:2 * A_KV].T.astype(BF16)
    vt_scr[:, WINDOW:WINDOW + ATTN_QB] = kvo_ref[0, :, A_KV:2 * A_KV].T.astype(BF16)
    band = WINDOW + CHUNK
    n_missing = jnp.where(pl.program_id(1) == 0, WINDOW, 0)
    problems = []
    for i in range(ATTN_QB // CHUNK):
        first_valid = (n_missing - i * CHUNK) if i * CHUNK < WINDOW else None
        for g in range(A_KV_HEADS):
            problems.append((i * CHUNK, CHUNK, i * CHUNK, band, g, first_valid))
    _attn_groups(sinks_ref, q_ref, o_ref, kv_scr, vt_scr, problems)


def _attn_prompt(q, kv, sinks):
    nb, t, _ = q.shape
    per_prev = ATTN_QB // WINDOW
    return pl.pallas_call(
        _attn_prompt_body,
        grid=(nb, t // ATTN_QB),
        in_specs=[
            pl.BlockSpec(memory_space=pltpu.SMEM),
            pl.BlockSpec((1, ATTN_QB, D_MODEL), lambda b, i: (b, i, 0)),
            pl.BlockSpec((1, WINDOW, 2 * A_KV), lambda b, i: (b, jnp.maximum(i * per_prev - 1, 0), 0)),
            pl.BlockSpec((1, ATTN_QB, 2 * A_KV), lambda b, i: (b, i, 0)),
        ],
        out_specs=pl.BlockSpec((1, ATTN_QB, D_MODEL), lambda b, i: (b, i, 0)),
        out_shape=jax.ShapeDtypeStruct((nb, t, D_MODEL), BF16),
        scratch_shapes=[pltpu.VMEM((WINDOW + ATTN_QB, 2 * A_KV), BF16),
                        pltpu.VMEM((A_KV, WINDOW + ATTN_QB), BF16)],
        compiler_params=_params(2, 32),
        name="swa_prompt_attn",
    )(sinks, q, kv, kv)


def _attn_sample_body(sinks_ref, q_ref, ck_ref, cv_ref, kv_ref, o_ref, kv_scr):
    rows = ck_ref.shape[1]
    t = q_ref.shape[1]
    kv_scr[0:rows, 0:A_KV] = ck_ref[0].astype(BF16)
    kv_scr[0:rows, A_KV:2 * A_KV] = cv_ref[0].astype(BF16)
    kv_scr[rows:rows + t, :] = kv_ref[0].astype(BF16)
    _attn_groups(sinks_ref, q_ref, o_ref, kv_scr, None,
                 [(0, t, 0, rows + t, g, None) for g in range(A_KV_HEADS)])


def _attn_sample(q, cache_k, cache_v, kv, sinks):
    nb, t, _ = q.shape
    rows = cache_k.shape[1]
    return pl.pallas_call(
        _attn_sample_body,
        grid=(nb,),
        in_specs=[
            pl.BlockSpec(memory_space=pltpu.SMEM),
            pl.BlockSpec((1, t, D_MODEL), lambda b: (b, 0, 0)),
            pl.BlockSpec((1, rows, A_KV), lambda b: (b, 0, 0)),
            pl.BlockSpec((1, rows, A_KV), lambda b: (b, 0, 0)),
            pl.BlockSpec((1, t, 2 * A_KV), lambda b: (b, 0, 0)),
        ],
        out_specs=pl.BlockSpec((1, t, D_MODEL), lambda b: (b, 0, 0)),
        out_shape=jax.ShapeDtypeStruct((nb, t, D_MODEL), BF16),
        scratch_shapes=[pltpu.VMEM((rows + t, 2 * A_KV), BF16)],
        compiler_params=_params(1, 32),
        name="swa_sample_attn",
    )(sinks, q, cache_k, cache_v, kv)


def _rope_tables(pos):
    inv = ROPE_THETA ** (-jnp.arange(0, A_HD, 2, dtype=F32) / A_HD)
    ang = pos.astype(F32)[:, None] * inv[None, :]
    cos = jnp.cos(ang)
    sin = jnp.sin(ang)
    reps = LANES // A_HD
    return (jnp.concatenate([cos, cos] * reps, axis=-1),
            jnp.concatenate([-sin, sin] * reps, axis=-1))


def kernel(x_prompt, x_sample, state_mlstm_C, state_mlstm_n, state_mlstm_m, cache_swa_k, cache_swa_v,
           ffn_norm1, ffn_w_in1, ffn_w_out1, mix_norm, mlstm_w_in, mlstm_b_gates, mlstm_head_norm,
           mlstm_w_out, swa_w_qkv, swa_sinks, swa_w_out, ffn_norm2, ffn_w_in2, ffn_w_out2, final_norm):
    bp, tp, _ = x_prompt.shape
    bs, ts, _ = x_sample.shape
    tm_p = 512
    tm_s = bs * ts

    w_in1, w_out1 = ffn_w_in1.astype(BF16), ffn_w_out1.astype(BF16)
    w_in2, w_out2 = ffn_w_in2.astype(BF16), ffn_w_out2.astype(BF16)
    m_w_main = mlstm_w_in[:, :, :M_MAIN].astype(BF16)
    n_gates = 2 * M_HEADS
    m_w_gates = jnp.pad(mlstm_w_in[:, :, M_MAIN:], ((0, 0), (0, 0), (0, LANES - n_gates))).astype(BF16)
    m_b_gates = jnp.pad(mlstm_b_gates.astype(F32), ((0, 0), (0, LANES - n_gates)))[:, None, :]
    m_w_out = mlstm_w_out.astype(BF16)
    s_wq = swa_w_qkv[:, :, :D_MODEL].astype(BF16)
    s_wkv = swa_w_qkv[:, :, D_MODEL:].astype(BF16)
    s_w_out = swa_w_out.astype(BF16)
    sinks = swa_sinks.astype(F32)

    cos_p, sin_p = _rope_tables(jnp.arange(tp))
    cos_s, sin_s = _rope_tables(PAST_LEN + jnp.arange(ts))
    cos_s, sin_s = jnp.tile(cos_s, (bs, 1)), jnp.tile(sin_s, (bs, 1))

    row = lambda a: a.astype(F32)[None, :]
    yp = x_prompt.reshape(bp * tp, D_MODEL)
    ys = x_sample.reshape(bs * ts, D_MODEL)
    zc = jnp.zeros((bp, M_HEADS, M_DK, M_DV), F32)
    zn = jnp.zeros((bp, M_HEADS, M_DK), F32)
    zm = jnp.zeros((bp, M_HEADS, LANES), F32)
    gf = row(final_norm)

    p_c, p_n, p_m, p_k, p_v = [], [], [], [], []
    s_c, s_n, s_m, s_k, s_v = [], [], [], [], []
    for i in range(DEPTH):
        j = i // 2
        yp = _ffn(yp, row(ffn_norm1[i]), w_in1[i], w_out1[i], gf, tm=tm_p, final=False)
        ys = _ffn(ys, row(ffn_norm1[i]), w_in1[i], w_out1[i], gf, tm=tm_s, final=False)
        gmix = row(mix_norm[i])
        if i % 2 == 0:
            hnorm = row(mlstm_head_norm[j])
            qkvo, gates = _mproj(yp, gmix, m_w_main[j], m_w_gates[j], m_b_gates[j], tm=tm_p)
            hg, c, n, m = _mlstm(qkvo.reshape(bp, tp, M_MAIN), gates.reshape(bp, tp, LANES),
                                 zc, zn, zm, hnorm, bg=2, L=min(M_SCAN_CHUNK, tp))
            mix_p = (hg.reshape(bp * tp, M_V), m_w_out[j])
            p_c.append(c); p_n.append(n); p_m.append(m[:, :, 0])

            qkvo, gates = _mproj(ys, gmix, m_w_main[j], m_w_gates[j], m_b_gates[j], tm=tm_s)
            m0 = jnp.broadcast_to(state_mlstm_m[j].astype(F32)[:, :, None], (bs, M_HEADS, LANES))
            hg, c, n, m = _mlstm(qkvo.reshape(bs, ts, M_MAIN), gates.reshape(bs, ts, LANES),
                                 state_mlstm_C[j].astype(F32), state_mlstm_n[j].astype(F32), m0,
                                 hnorm, bg=bs, L=min(CHUNK, ts))
            mix_s = (hg.reshape(bs * ts, M_V), m_w_out[j])
            s_c.append(c); s_n.append(n); s_m.append(m[:, :, 0])
        else:
            q, kv = _sproj(yp, gmix, s_wq[j], s_wkv[j], cos_p, sin_p, tm=tm_p)
            kv3 = kv.reshape(bp, tp, 2 * A_KV)
            o = _attn_prompt(q.reshape(bp, tp, D_MODEL), kv3, sinks[j])
            mix_p = (o.reshape(bp * tp, D_MODEL), s_w_out[j])
            keep = min(WINDOW, tp)
            p_k.append(kv3[:, tp - keep:, :A_KV].reshape(bp, keep, A_KV_HEADS, A_HD))
            p_v.append(kv3[:, tp - keep:, A_KV:].reshape(bp, keep, A_KV_HEADS, A_HD))

            q, kv = _sproj(ys, gmix, s_wq[j], s_wkv[j], cos_s, sin_s, tm=tm_s)
            kv3 = kv.reshape(bs, ts, 2 * A_KV)
            rows = cache_swa_k.shape[2]
            o = _attn_sample(q.reshape(bs, ts, D_MODEL),
                             cache_swa_k[j].astype(F32).reshape(bs, rows, A_KV),
                             cache_swa_v[j].astype(F32).reshape(bs, rows, A_KV), kv3, sinks[j])
            mix_s = (o.reshape(bs * ts, D_MODEL), s_w_out[j])
            s_k.append(kv3[:, :, :A_KV].reshape(bs, ts, A_KV_HEADS, A_HD))
            s_v.append(kv3[:, :, A_KV:].reshape(bs, ts, A_KV_HEADS, A_HD))
        last = i == DEPTH - 1
        yp = _ffn(yp, row(ffn_norm2[i]), w_in2[i], w_out2[i], gf, tm=tm_p, final=last, mixer_out=mix_p)
        ys = _ffn(ys, row(ffn_norm2[i]), w_in2[i], w_out2[i], gf, tm=tm_s, final=last, mixer_out=mix_s)

    return (yp.reshape(bp, tp, D_MODEL), ys.reshape(bs, ts, D_MODEL),
            jnp.stack(p_c), jnp.stack(p_n), jnp.stack(p_m), jnp.stack(p_k), jnp.stack(p_v),
            jnp.stack(s_c), jnp.stack(s_n), jnp.stack(s_m), jnp.stack(s_k), jnp.stack(s_v))
```

```python
import functools

import jax
import jax.numpy as jnp
from jax import lax
from jax.experimental import pallas as pl
from jax.experimental.pallas import tpu as pltpu

F32 = jnp.float32
BF16 = jnp.bfloat16

D_MODEL = 1024
DEPTH = 4
CHUNK = 64
M_HEADS = 4
M_DK = D_MODEL // 8
M_DV = D_MODEL // M_HEADS
M_QK = M_HEADS * M_DK
M_V = M_HEADS * M_DV
M_MAIN = 2 * M_QK + 2 * M_V
A_HEADS = 16
A_KV_HEADS = 4
A_HD = D_MODEL // A_HEADS
A_GROUP = A_HEADS // A_KV_HEADS
A_KV = A_KV_HEADS * A_HD
WINDOW = 128
PAST_LEN = 4096
ROPE_THETA = 10000.0
D_FF = 11 * D_MODEL // 4
FFN_RES = 0.5
EPS = 1e-6

LANES = 128
FFN_CHUNK = 256
PROJ_CHUNK = 512
ATTN_QB = 256
M_SCAN_CHUNK = 256
MIB = 1024 * 1024


def _rms(x, g):
    return x * lax.rsqrt(jnp.mean(x * x, axis=-1, keepdims=True) + EPS) * g


def _const_spec(shape):
    nd = len(shape)
    return pl.BlockSpec(shape, lambda *_: (0,) * nd, pipeline_mode=pl.Buffered(1))


def _layer_spec(shape, layer):
    nd = len(shape)
    return pl.BlockSpec((None, *shape), lambda *_: (layer,) + (0,) * nd, pipeline_mode=pl.Buffered(1))


def _params(n_grid, vmem_mib):
    return pltpu.CompilerParams(
        dimension_semantics=("arbitrary",) * n_grid,
        vmem_limit_bytes=vmem_mib * MIB)


def _mlstm_in_proj(xn, w_ref, wg_ref, bg_ref, o_ref, og_ref):
    for c in range(M_MAIN // PROJ_CHUNK):
        lo = c * PROJ_CHUNK
        o_ref[:, lo:lo + PROJ_CHUNK] = jnp.dot(
            xn, w_ref[:, lo:lo + PROJ_CHUNK], preferred_element_type=F32).astype(BF16)
    og_ref[...] = jnp.dot(xn, wg_ref[...], preferred_element_type=F32) + bg_ref[...]


def _rope(x, cos, sin_signed, first_half):
    swapped = jnp.where(first_half, pltpu.roll(x, LANES - A_HD // 2, 1), pltpu.roll(x, A_HD // 2, 1))
    return x * cos + swapped * sin_signed


def _swa_in_proj(xn, wq_ref, wkv_ref, cos_ref, sin_ref, q_ref, kv_ref):
    cos = cos_ref[...]
    sin_signed = sin_ref[...]
    lane = lax.broadcasted_iota(jnp.int32, cos.shape, 1)
    first_half = (lane & (A_HD // 2)) == 0
    q_scale = A_HD ** -0.5
    for c in range(D_MODEL // PROJ_CHUNK):
        q = jnp.dot(xn, wq_ref[:, c * PROJ_CHUNK:(c + 1) * PROJ_CHUNK], preferred_element_type=F32)
        for j in range(PROJ_CHUNK // LANES):
            lo = c * PROJ_CHUNK + j * LANES
            blk = _rope(q[:, j * LANES:(j + 1) * LANES], cos, sin_signed, first_half)
            q_ref[:, lo:lo + LANES] = (blk * q_scale).astype(BF16)
    kv = jnp.dot(xn, wkv_ref[...], preferred_element_type=F32)
    for j in range(A_KV // LANES):
        kv_ref[:, j * LANES:(j + 1) * LANES] = _rope(
            kv[:, j * LANES:(j + 1) * LANES], cos, sin_signed, first_half)
    kv_ref[:, A_KV:2 * A_KV] = kv[:, A_KV:2 * A_KV]


def _ffn_body(*refs, mixer_out, final, proj):
    refs = iter(refs)
    x_ref = next(refs)
    if mixer_out:
        a_ref, wmix_ref = next(refs), next(refs)
        x = x_ref[...] + jnp.dot(a_ref[...], wmix_ref[...], preferred_element_type=F32)
    else:
        x = x_ref[...]
    g_ref, win_ref, wout_ref, gf_ref = next(refs), next(refs), next(refs), next(refs)
    if proj:
        gmix_ref = next(refs)
        proj_refs = [next(refs) for _ in range(3 if proj == "mlstm" else 4)]
    o_ref = next(refs)
    if proj:
        proj_refs += [next(refs), next(refs)]
    h_ref = next(refs)

    xn = _rms(x, g_ref[...]).astype(BF16)
    for c in range(D_FF // FFN_CHUNK):
        lo = c * FFN_CHUNK
        gate = jnp.dot(xn, win_ref[:, lo:lo + FFN_CHUNK], preferred_element_type=F32)
        up = jnp.dot(xn, win_ref[:, D_FF + lo:D_FF + lo + FFN_CHUNK], preferred_element_type=F32)
        h_ref[:, lo:lo + FFN_CHUNK] = (gate * jax.nn.sigmoid(gate) * up).astype(BF16)
    y = x + FFN_RES * jnp.dot(h_ref[...], wout_ref[...], preferred_element_type=F32)
    if final:
        o_ref[...] = _rms(y, gf_ref[...])
        return
    o_ref[...] = y
    if proj:
        yn = _rms(y, gmix_ref[...]).astype(BF16)
        (_mlstm_in_proj if proj == "mlstm" else _swa_in_proj)(yn, *proj_refs)


def _ffn(x, g, w_in, w_out, g_final, *, layer, tm, final=False, mixer_out=None, proj=None):
    n = x.shape[0]
    tile = lambda width: pl.BlockSpec((tm, width), lambda i: (i, 0))
    args, specs = [x], [tile(D_MODEL)]
    if mixer_out:
        a, w_mix, mix_layer = mixer_out
        args += [a, w_mix]
        specs += [tile(D_MODEL), _layer_spec((D_MODEL, D_MODEL), mix_layer)]
    args += [g, w_in, w_out, g_final]
    specs += [_layer_spec((1, D_MODEL), layer), _layer_spec((D_MODEL, 2 * D_FF), layer),
              _layer_spec((D_FF, D_MODEL), layer), _const_spec((1, D_MODEL))]
    out_specs = [tile(D_MODEL)]
    out_shape = [jax.ShapeDtypeStruct((n, D_MODEL), F32)]
    kind = None
    if proj:
        kind, gmix, norm_layer = proj[:3]
        args.append(gmix)
        specs.append(_layer_spec((1, D_MODEL), norm_layer))
        if kind == "mlstm":
            w, wg, bg, mix_layer = proj[3:]
            args += [w, wg, bg]
            specs += [_layer_spec((D_MODEL, M_MAIN), mix_layer), _layer_spec((D_MODEL, LANES), mix_layer),
                      _layer_spec((1, LANES), mix_layer)]
            widths = [(M_MAIN, BF16), (LANES, F32)]
        else:
            wq, wkv, cos, sin_signed, mix_layer = proj[3:]
            n_tab = cos.shape[0] // tm
            table = pl.BlockSpec((tm, LANES), lambda i: (i % n_tab, 0))
            args += [wq, wkv, cos, sin_signed]
            specs += [_layer_spec((D_MODEL, D_MODEL), mix_layer), _layer_spec((D_MODEL, 2 * A_KV), mix_layer),
                      table, table]
            widths = [(D_MODEL, BF16), (2 * A_KV, F32)]
        out_specs += [tile(wd) for wd, _ in widths]
        out_shape += [jax.ShapeDtypeStruct((n, wd), dt) for wd, dt in widths]
    return pl.pallas_call(
        functools.partial(_ffn_body, mixer_out=bool(mixer_out), final=final, proj=kind),
        grid=(n // tm,),
        in_specs=specs,
        out_specs=out_specs,
        out_shape=out_shape,
        scratch_shapes=[pltpu.VMEM((tm, D_FF), BF16)],
        compiler_params=_params(1, 56),
        name="half_ffn",
    )(*args)


def _mlstm_body(qkvo_ref, gates_ref, c0_ref, n0_ref, m0_ref, hn_ref,
                hg_ref, c_ref, n_ref, m_ref, *, bg, L):
    @pl.when(pl.program_id(1) == 0)
    def _():
        c_ref[...] = c0_ref[...]
        n_ref[...] = n0_ref[...]
        m_ref[...] = m0_ref[...]

    scale = M_DK ** -0.5
    row = lax.broadcasted_iota(jnp.int32, (L, L), 0)
    col = lax.broadcasted_iota(jnp.int32, (L, L), 1)
    causal = col <= row
    tril = causal.astype(BF16)
    lane = lax.broadcasted_iota(jnp.int32, (L, LANES), 1)

    probs = []
    for b in range(bg):
        gts = gates_ref[b]
        lf = jax.nn.log_sigmoid(gts)
        lf_hi = lf.astype(BF16)
        r1 = lf - lf_hi.astype(F32)
        lf_mid = r1.astype(BF16)
        lf_lo = (r1 - lf_mid.astype(F32)).astype(BF16)
        bcs = (jnp.dot(tril, lf_hi, preferred_element_type=F32)
               + jnp.dot(tril, lf_mid, preferred_element_type=F32)
               + jnp.dot(tril, lf_lo, preferred_element_type=F32))
        both = jnp.where(lane < M_HEADS, gts, bcs)
        if L % LANES:
            both = jnp.concatenate([both, jnp.zeros((LANES - L, LANES), F32)], axis=0)
        both_t = both.T
        for h in range(M_HEADS):
            p = dict(b=b, h=h)
            b_row = both_t[M_HEADS + h:M_HEADS + h + 1, 0:L]
            p["b_last"] = b_row[:, L - 1:L]
            p["bmi_row"] = b_row - both_t[h:h + 1, 0:L]
            b_col = bcs[:, M_HEADS + h:M_HEADS + h + 1]
            p["b_rep"] = jnp.broadcast_to(b_col, (L, LANES))
            p["c_rep"] = jnp.broadcast_to(gts[:, h:h + 1] - b_col, (L, LANES))
            p["m_prev"] = m_ref[b, h:h + 1, 0:1]
            p["n_row"] = n_ref[b, h:h + 1, :]
            p["c_prev"] = c_ref[b, h]
            p["q"] = qkvo_ref[b, :, h * M_DK:(h + 1) * M_DK]
            p["k"] = qkvo_ref[b, :, M_QK + h * M_DK:M_QK + (h + 1) * M_DK]
            p["v"] = qkvo_ref[b, :, 2 * M_QK + h * M_DV:2 * M_QK + (h + 1) * M_DV]
            probs.append(p)

    def wide(c, width):
        return c[:, :width] if width <= LANES else jnp.concatenate([c] * (width // LANES), axis=1)

    def fold(x):
        acc = x[:, :LANES]
        for t in range(1, x.shape[1] // LANES):
            acc = acc + x[:, t * LANES:(t + 1) * LANES]
        return acc

    for p in probs:
        p["s"] = lax.dot_general(p["q"], p["k"], (((1,), (1,)), ((), ())), preferred_element_type=F32)
        p["qc"] = jnp.dot(p["q"], p["c_prev"].astype(BF16), preferred_element_type=F32)
        n_rep = jnp.broadcast_to(p["n_row"], (LANES, M_DK)).astype(BF16)
        p["qn"] = lax.dot_general(p["q"], n_rep, (((1,), (1,)), ((), ())), preferred_element_type=F32)

    for p in probs:
        d = jnp.where(causal, wide(p["b_rep"], L) - p["bmi_row"], -jnp.inf)
        g_rep = p["b_rep"] + p["m_prev"]
        m_t = jnp.maximum(g_rep, jnp.max(d, axis=-1, keepdims=True))
        w = jnp.exp(d - wide(m_t, L)) * (p["s"] * scale)
        p["m_t"] = m_t
        p["inter"] = jnp.exp(g_rep - m_t)
        p["w_sum"] = jnp.sum(fold(w), axis=-1, keepdims=True)
        p["w"] = w.astype(BF16)
        b_last = p["b_last"]
        a_row = b_last - p["bmi_row"]
        m_new = jnp.maximum(p["m_prev"] + b_last, jnp.max(a_row, axis=-1, keepdims=True))
        p["m_new"] = m_new
        p["decay"] = jnp.exp(p["m_prev"] + b_last - m_new)
        p["wk"] = (jnp.exp(b_last + p["c_rep"] - m_new) * scale) * p["k"].astype(F32)

    for p in probs:
        p["wv"] = jnp.dot(p["w"], p["v"], preferred_element_type=F32)
        p["kv"] = lax.dot_general(p["wk"].astype(BF16), p["v"], (((0,), (0,)), ((), ())),
                                  preferred_element_type=F32)

    for p in probs:
        b, h = p["b"], p["h"]
        num = wide(p["inter"], M_DV) * p["qc"] + p["wv"]
        den = p["inter"] * p["qn"] + p["w_sum"]
        hh = num / wide(jnp.maximum(jnp.abs(den), jnp.exp(-p["m_t"])), M_DV)
        hh = hh * lax.rsqrt(jnp.mean(hh * hh, axis=-1, keepdims=True) + EPS)
        hh = hh * hn_ref[:, h * M_DV:(h + 1) * M_DV]
        og = qkvo_ref[b, :, 2 * M_QK + M_V + h * M_DV:2 * M_QK + M_V + (h + 1) * M_DV]
        hg_ref[b, :, h * M_DV:(h + 1) * M_DV] = (jax.nn.sigmoid(og.astype(F32)) * hh).astype(BF16)
        c_ref[b, h] = p["decay"] * p["c_prev"] + p["kv"]
        n_ref[b, h:h + 1, :] = p["decay"] * p["n_row"] + jnp.sum(p["wk"], axis=0, keepdims=True)
        m_ref[b, h:h + 1, :] = jnp.broadcast_to(p["m_new"], (1, LANES))


def _mlstm(qkvo, gates, c0, n0, m0, hnorm, *, layer, state_layer, bg, L):
    nb, t, _ = qkvo.shape
    state_shapes = [(bg, M_HEADS, M_DK, M_DV), (bg, M_HEADS, M_DK), (bg, M_HEADS, LANES)]
    state_specs = [pl.BlockSpec(shp, lambda g, c, nd=len(shp): (g,) + (0,) * (nd - 1))
                   for shp in state_shapes]
    if state_layer is None:
        init_specs = state_specs
    else:
        init_specs = [pl.BlockSpec((None, *shp), lambda g, c, nd=len(shp): (state_layer, g) + (0,) * (nd - 1))
                      for shp in state_shapes]
    return pl.pallas_call(
        functools.partial(_mlstm_body, bg=bg, L=L),
        grid=(nb // bg, t // L),
        in_specs=[
            pl.BlockSpec((bg, L, M_MAIN), lambda g, c: (g, c, 0)),
            pl.BlockSpec((bg, L, LANES), lambda g, c: (g, c, 0)),
            *init_specs,
            _layer_spec((1, M_V), layer),
        ],
        out_specs=[pl.BlockSpec((bg, L, M_V), lambda g, c: (g, c, 0)), *state_specs],
        out_shape=[
            jax.ShapeDtypeStruct((nb, t, M_V), BF16),
            jax.ShapeDtypeStruct((nb, M_HEADS, M_DK, M_DV), F32),
            jax.ShapeDtypeStruct((nb, M_HEADS, M_DK), F32),
            jax.ShapeDtypeStruct((nb, M_HEADS, LANES), F32),
        ],
        compiler_params=_params(2, 32),
        name="mlstm_scan",
    )(qkvo, gates, c0, n0, m0, hnorm)


def _attn_groups(sinks_ref, q_ref, o_ref, kv_scr, vt_scr, problems):
    scores = []
    for q_lo, q_rows, k_lo, k_rows, g, first_valid in problems:
        kb = kv_scr[k_lo:k_lo + k_rows, g * A_HD:(g + 1) * A_HD]
        qs = jnp.concatenate(
            [q_ref[0, q_lo:q_lo + q_rows, hd * A_HD:(hd + 1) * A_HD]
             for hd in range(A_GROUP * g, A_GROUP * (g + 1))], axis=0)
        scores.append(lax.dot_general(kb, qs, (((1,), (1,)), ((), ())), preferred_element_type=F32))
    probs = []
    for s, (q_lo, q_rows, k_lo, k_rows, g, first_valid) in zip(scores, problems):
        if first_valid is not None:
            key = lax.broadcasted_iota(jnp.int32, s.shape, 0)
            s = jnp.where(key >= first_valid, s, -jnp.inf)
        sk = jnp.concatenate([jnp.full((1, q_rows), sinks_ref[hd], F32)
                              for hd in range(A_GROUP * g, A_GROUP * (g + 1))], axis=1)
        m = jnp.maximum(jnp.max(s, axis=0, keepdims=True), sk)
        p = jnp.exp(s - m)
        den = jnp.sum(p, axis=0, keepdims=True) + jnp.exp(sk - m)
        probs.append((p.astype(BF16), den))
    outs = []
    for (p, den), (q_lo, q_rows, k_lo, k_rows, g, first_valid) in zip(probs, problems):
        if vt_scr is None:
            vb = kv_scr[k_lo:k_lo + k_rows, A_KV + g * A_HD:A_KV + (g + 1) * A_HD]
            o_t = lax.dot_general(vb, p, (((0,), (0,)), ((), ())), preferred_element_type=F32)
        else:
            o_t = jnp.dot(vt_scr[g * A_HD:(g + 1) * A_HD, k_lo:k_lo + k_rows], p,
                          preferred_element_type=F32)
        outs.append(o_t / den)
    for o_t, (q_lo, q_rows, k_lo, k_rows, g, first_valid) in zip(outs, problems):
        o = o_t.T
        for j in range(A_GROUP):
            hd = A_GROUP * g + j
            o_ref[0, q_lo:q_lo + q_rows, hd * A_HD:(hd + 1) * A_HD] = (
                o[j * q_rows:(j + 1) * q_rows].astype(BF16))


def _attn_prompt_body(sinks_ref, q_ref, kvp_ref, kvo_ref, o_ref, kv_scr, vt_scr):
    kv_scr[0:WINDOW, :] = kvp_ref[0].astype(BF16)
    kv_scr[WINDOW:WINDOW + ATTN_QB, :] = kvo_ref[0].astype(BF16)
    vt_scr[:, 0:WINDOW] = kvp_ref[0, :, A_KV:2 * A_KV].T.astype(BF16)
    vt_scr[:, WINDOW:WINDOW + ATTN_QB] = kvo_ref[0, :, A_KV:2 * A_KV].T.astype(BF16)
    band = WINDOW + CHUNK
    n_missing = jnp.where(pl.program_id(1) == 0, WINDOW, 0)
    problems = []
    for i in range(ATTN_QB // CHUNK):
        first_valid = (n_missing - i * CHUNK) if i * CHUNK < WINDOW else None
        for g in range(A_KV_HEADS):
            problems.append((i * CHUNK, CHUNK, i * CHUNK, band, g, first_valid))
    _attn_groups(sinks_ref, q_ref, o_ref, kv_scr, vt_scr, problems)


def _attn_prompt(q, kv, sinks):
    nb, t, _ = q.shape
    per_prev = ATTN_QB // WINDOW
    return pl.pallas_call(
        _attn_prompt_body,
        grid=(nb, t // ATTN_QB),
        in_specs=[
            pl.BlockSpec(memory_space=pltpu.SMEM),
            pl.BlockSpec((1, ATTN_QB, D_MODEL), lambda b, i: (b, i, 0)),
            pl.BlockSpec((1, WINDOW, 2 * A_KV), lambda b, i: (b, jnp.maximum(i * per_prev - 1, 0), 0)),
            pl.BlockSpec((1, ATTN_QB, 2 * A_KV), lambda b, i: (b, i, 0)),
        ],
        out_specs=pl.BlockSpec((1, ATTN_QB, D_MODEL), lambda b, i: (b, i, 0)),
        out_shape=jax.ShapeDtypeStruct((nb, t, D_MODEL), BF16),
        scratch_shapes=[pltpu.VMEM((WINDOW + ATTN_QB, 2 * A_KV), BF16),
                        pltpu.VMEM((A_KV, WINDOW + ATTN_QB), BF16)],
        compiler_params=_params(2, 32),
        name="swa_prompt_attn",
    )(sinks, q, kv, kv)


def _attn_sample_body(sinks_ref, q_ref, ck_ref, cv_ref, kv_ref, o_ref, kv_scr):
    rows = ck_ref.shape[1]
    t = q_ref.shape[1]
    kv_scr[0:rows, 0:A_KV] = ck_ref[0].astype(BF16)
    kv_scr[0:rows, A_KV:2 * A_KV] = cv_ref[0].astype(BF16)
    kv_scr[rows:rows + t, :] = kv_ref[0].astype(BF16)
    _attn_groups(sinks_ref, q_ref, o_ref, kv_scr, None,
                 [(0, t, 0, rows + t, g, None) for g in range(A_KV_HEADS)])


def _attn_sample(q, cache_k, cache_v, kv, sinks, *, layer):
    nb, t, _ = q.shape
    rows = cache_k.shape[2]
    return pl.pallas_call(
        _attn_sample_body,
        grid=(nb,),
        in_specs=[
            pl.BlockSpec(memory_space=pltpu.SMEM),
            pl.BlockSpec((1, t, D_MODEL), lambda b: (b, 0, 0)),
            pl.BlockSpec((None, 1, rows, A_KV), lambda b: (layer, b, 0, 0)),
            pl.BlockSpec((None, 1, rows, A_KV), lambda b: (layer, b, 0, 0)),
            pl.BlockSpec((1, t, 2 * A_KV), lambda b: (b, 0, 0)),
        ],
        out_specs=pl.BlockSpec((1, t, D_MODEL), lambda b: (b, 0, 0)),
        out_shape=jax.ShapeDtypeStruct((nb, t, D_MODEL), BF16),
        scratch_shapes=[pltpu.VMEM((rows + t, 2 * A_KV), BF16)],
        compiler_params=_params(1, 32),
        name="swa_sample_attn",
    )(sinks, q, cache_k, cache_v, kv)


def _rope_tables(pos):
    inv = ROPE_THETA ** (-jnp.arange(0, A_HD, 2, dtype=F32) / A_HD)
    ang = pos.astype(F32)[:, None] * inv[None, :]
    cos = jnp.cos(ang)
    sin = jnp.sin(ang)
    reps = LANES // A_HD
    return (jnp.concatenate([cos, cos] * reps, axis=-1),
            jnp.concatenate([-sin, sin] * reps, axis=-1))


def kernel(x_prompt, x_sample, state_mlstm_C, state_mlstm_n, state_mlstm_m, cache_swa_k, cache_swa_v,
           ffn_norm1, ffn_w_in1, ffn_w_out1, mix_norm, mlstm_w_in, mlstm_b_gates, mlstm_head_norm,
           mlstm_w_out, swa_w_qkv, swa_sinks, swa_w_out, ffn_norm2, ffn_w_in2, ffn_w_out2, final_norm):
    bp, tp, _ = x_prompt.shape
    bs, ts, _ = x_sample.shape
    tm_p = 512
    tm_s = bs * ts

    w_in1, w_out1 = ffn_w_in1.astype(BF16), ffn_w_out1.astype(BF16)
    w_in2, w_out2 = ffn_w_in2.astype(BF16), ffn_w_out2.astype(BF16)
    m_w_main = mlstm_w_in[:, :, :M_MAIN].astype(BF16)
    n_gates = 2 * M_HEADS
    m_w_gates = jnp.pad(mlstm_w_in[:, :, M_MAIN:], ((0, 0), (0, 0), (0, LANES - n_gates))).astype(BF16)
    m_b_gates = jnp.pad(mlstm_b_gates.astype(F32), ((0, 0), (0, LANES - n_gates)))[:, None, :]
    m_w_out = mlstm_w_out.astype(BF16)
    s_wq = swa_w_qkv[:, :, :D_MODEL].astype(BF16)
    s_wkv = swa_w_qkv[:, :, D_MODEL:].astype(BF16)
    s_w_out = swa_w_out.astype(BF16)
    sinks = swa_sinks.astype(F32)
    rows3 = lambda a: a.astype(F32)[:, None, :]
    g1, g2, gmix, hnorm = rows3(ffn_norm1), rows3(ffn_norm2), rows3(mix_norm), rows3(mlstm_head_norm)
    gf = final_norm.astype(F32)[None, :]

    cos_p, sin_p = _rope_tables(jnp.arange(tp))
    cos_s, sin_s = _rope_tables(PAST_LEN + jnp.arange(ts))
    cos_s, sin_s = jnp.tile(cos_s, (bs, 1)), jnp.tile(sin_s, (bs, 1))

    yp = x_prompt.reshape(bp * tp, D_MODEL)
    ys = x_sample.reshape(bs * ts, D_MODEL)
    zc = jnp.zeros((bp, M_HEADS, M_DK, M_DV), F32)
    zn = jnp.zeros((bp, M_HEADS, M_DK), F32)
    zm = jnp.zeros((bp, M_HEADS, LANES), F32)
    c0_s = state_mlstm_C.astype(F32)
    n0_s = state_mlstm_n.astype(F32)
    m0_s = jnp.broadcast_to(state_mlstm_m.astype(F32)[..., None], state_mlstm_m.shape + (LANES,))
    rows = cache_swa_k.shape[2]
    ck = cache_swa_k.astype(F32).reshape(-1, bs, rows, A_KV)
    cv = cache_swa_v.astype(F32).reshape(-1, bs, rows, A_KV)

    p_c, p_n, p_m, p_k, p_v = [], [], [], [], []
    s_c, s_n, s_m, s_k, s_v = [], [], [], [], []
    for i in range(DEPTH):
        j = i // 2
        if i % 2 == 0:
            proj_p = proj_s = ("mlstm", gmix, i, m_w_main, m_w_gates, m_b_gates, j)
        else:
            proj_p = ("swa", gmix, i, s_wq, s_wkv, cos_p, sin_p, j)
            proj_s = ("swa", gmix, i, s_wq, s_wkv, cos_s, sin_s, j)
        yp, pa_p, pb_p = _ffn(yp, g1, w_in1, w_out1, gf, layer=i, tm=tm_p, proj=proj_p)
        ys, pa_s, pb_s = _ffn(ys, g1, w_in1, w_out1, gf, layer=i, tm=tm_s, proj=proj_s)
        if i % 2 == 0:
            qkvo, gates = pa_p, pb_p
            hg, c, n, m = _mlstm(qkvo.reshape(bp, tp, M_MAIN), gates.reshape(bp, tp, LANES),
                                 zc, zn, zm, hnorm, layer=j, state_layer=None, bg=4,
                                 L=min(M_SCAN_CHUNK, tp))
            mix_p = (hg.reshape(bp * tp, M_V), m_w_out, j)
            p_c.append(c); p_n.append(n); p_m.append(m[:, :, 0])

            qkvo, gates = pa_s, pb_s
            hg, c, n, m = _mlstm(qkvo.reshape(bs, ts, M_MAIN), gates.reshape(bs, ts, LANES),
                                 c0_s, n0_s, m0_s, hnorm, layer=j, state_layer=j, bg=bs,
                                 L=min(CHUNK, ts))
            mix_s = (hg.reshape(bs * ts, M_V), m_w_out, j)
            s_c.append(c); s_n.append(n); s_m.append(m[:, :, 0])
        else:
            q, kv = pa_p, pb_p
            kv3 = kv.reshape(bp, tp, 2 * A_KV)
            o = _attn_prompt(q.reshape(bp, tp, D_MODEL), kv3, sinks[j])
            mix_p = (o.reshape(bp * tp, D_MODEL), s_w_out, j)
            keep = min(WINDOW, tp)
            p_k.append(kv3[:, tp - keep:, :A_KV].reshape(bp, keep, A_KV_HEADS, A_HD))
            p_v.append(kv3[:, tp - keep:, A_KV:].reshape(bp, keep, A_KV_HEADS, A_HD))

            q, kv = pa_s, pb_s
            kv3 = kv.reshape(bs, ts, 2 * A_KV)
            o = _attn_sample(q.reshape(bs, ts, D_MODEL), ck, cv, kv3, sinks[j], layer=j)
            mix_s = (o.reshape(bs * ts, D_MODEL), s_w_out, j)
            s_k.append(kv3[:, :, :A_KV].reshape(bs, ts, A_KV_HEADS, A_HD))
            s_v.append(kv3[:, :, A_KV:].reshape(bs, ts, A_KV_HEADS, A_HD))
        last = i == DEPTH - 1
        yp, = _ffn(yp, g2, w_in2, w_out2, gf, layer=i, tm=tm_p, final=last, mixer_out=mix_p)
        ys, = _ffn(ys, g2, w_in2, w_out2, gf, layer=i, tm=tm_s, final=last, mixer_out=mix_s)

    return (yp.reshape(bp, tp, D_MODEL), ys.reshape(bs, ts, D_MODEL),
            jnp.stack(p_c), jnp.stack(p_n), jnp.stack(p_m), jnp.stack(p_k), jnp.stack(p_v),
            jnp.stack(s_c), jnp.stack(s_n), jnp.stack(s_m), jnp.stack(s_k), jnp.stack(s_v))
```

```python
import functools

import jax
import jax.numpy as jnp
from jax import lax
from jax.experimental import pallas as pl
from jax.experimental.pallas import tpu as pltpu

F32 = jnp.float32
BF16 = jnp.bfloat16

D_MODEL = 1024
DEPTH = 4
CHUNK = 64
M_HEADS = 4
M_DK = D_MODEL // 8
M_DV = D_MODEL // M_HEADS
M_QK = M_HEADS * M_DK
M_V = M_HEADS * M_DV
M_MAIN = 2 * M_QK + 2 * M_V
A_HEADS = 16
A_KV_HEADS = 4
A_HD = D_MODEL // A_HEADS
A_GROUP = A_HEADS // A_KV_HEADS
A_KV = A_KV_HEADS * A_HD
WINDOW = 128
PAST_LEN = 4096
ROPE_THETA = 10000.0
D_FF = 11 * D_MODEL // 4
FFN_RES = 0.5
EPS = 1e-6

LANES = 128
FFN_CHUNK = 256
PROJ_CHUNK = 512
M_SCAN_CHUNK = 256
MIB = 1024 * 1024


def _rms(x, g):
    return x * lax.rsqrt(jnp.mean(x * x, axis=-1, keepdims=True) + EPS) * g


def _const_spec(shape):
    nd = len(shape)
    return pl.BlockSpec(shape, lambda *_: (0,) * nd, pipeline_mode=pl.Buffered(1))


def _layer_spec(shape, layer):
    nd = len(shape)
    return pl.BlockSpec((None, *shape), lambda *_: (layer,) + (0,) * nd, pipeline_mode=pl.Buffered(1))


def _params(n_grid, vmem_mib):
    return pltpu.CompilerParams(
        dimension_semantics=("arbitrary",) * n_grid,
        vmem_limit_bytes=vmem_mib * MIB)


def _mlstm_in_proj(xn, w_ref, wg_ref, bg_ref, o_ref, og_ref):
    for c in range(M_MAIN // PROJ_CHUNK):
        lo = c * PROJ_CHUNK
        o_ref[:, lo:lo + PROJ_CHUNK] = jnp.dot(
            xn, w_ref[:, lo:lo + PROJ_CHUNK], preferred_element_type=F32).astype(BF16)
    og_ref[...] = jnp.dot(xn, wg_ref[...], preferred_element_type=F32) + bg_ref[...]


def _rope(x, cos, sin_signed, first_half):
    swapped = jnp.where(first_half, pltpu.roll(x, LANES - A_HD // 2, 1), pltpu.roll(x, A_HD // 2, 1))
    return x * cos + swapped * sin_signed


def _swa_in_proj(xn, wq_ref, wkv_ref, cos_ref, sin_ref, q_ref, kv_ref):
    cos = cos_ref[...]
    sin_signed = sin_ref[...]
    lane = lax.broadcasted_iota(jnp.int32, cos.shape, 1)
    first_half = (lane & (A_HD // 2)) == 0
    q_scale = A_HD ** -0.5
    for c in range(D_MODEL // PROJ_CHUNK):
        q = jnp.dot(xn, wq_ref[:, c * PROJ_CHUNK:(c + 1) * PROJ_CHUNK], preferred_element_type=F32)
        for j in range(PROJ_CHUNK // LANES):
            lo = c * PROJ_CHUNK + j * LANES
            blk = _rope(q[:, j * LANES:(j + 1) * LANES], cos, sin_signed, first_half)
            q_ref[:, lo:lo + LANES] = (blk * q_scale).astype(BF16)
    kv = jnp.dot(xn, wkv_ref[...], preferred_element_type=F32)
    for j in range(A_KV // LANES):
        kv_ref[:, j * LANES:(j + 1) * LANES] = _rope(
            kv[:, j * LANES:(j + 1) * LANES], cos, sin_signed, first_half)
    kv_ref[:, A_KV:2 * A_KV] = kv[:, A_KV:2 * A_KV]


def _ffn_body(*refs, mixer_out, final, proj, attn):
    refs = iter(refs)
    x_ref = next(refs)
    if attn:
        q_ref, kvp_ref, kvo_ref, sinks_ref, wmix_ref = [next(refs) for _ in range(5)]
    elif mixer_out:
        a_ref, wmix_ref = next(refs), next(refs)
    g_ref, win_ref, wout_ref, gf_ref = next(refs), next(refs), next(refs), next(refs)
    if proj:
        gmix_ref = next(refs)
        proj_refs = [next(refs) for _ in range(3 if proj == "mlstm" else 4)]
    o_ref = next(refs)
    if proj:
        proj_refs += [next(refs), next(refs)]
    h_ref = next(refs)

    if attn:
        n_tiles, tiles_per_seq = attn
        a_scr, kv_scr, vt_scr = next(refs), next(refs), next(refs)
        s = pl.program_id(0)

        @pl.when(s == 0)
        def _():
            a_scr[...] = jnp.zeros_like(a_scr)

        slot = lax.rem(s, 2)
        a = a_scr[1 - slot]
        x = x_ref[...] + jnp.dot(a, wmix_ref[...], preferred_element_type=F32)
        first_of_seq = lax.rem(jnp.minimum(s, n_tiles - 1), tiles_per_seq) == 0
        _attn_prompt_tile(sinks_ref, q_ref, kvp_ref, kvo_ref, a_scr.at[slot], kv_scr, vt_scr, first_of_seq)
    elif mixer_out:
        x = x_ref[...] + jnp.dot(a_ref[...], wmix_ref[...], preferred_element_type=F32)
    else:
        x = x_ref[...]

    xn = _rms(x, g_ref[...]).astype(BF16)
    for c in range(D_FF // FFN_CHUNK):
        lo = c * FFN_CHUNK
        gate = jnp.dot(xn, win_ref[:, lo:lo + FFN_CHUNK], preferred_element_type=F32)
        up = jnp.dot(xn, win_ref[:, D_FF + lo:D_FF + lo + FFN_CHUNK], preferred_element_type=F32)
        h_ref[:, lo:lo + FFN_CHUNK] = (gate * jax.nn.sigmoid(gate) * up).astype(BF16)
    y = x + FFN_RES * jnp.dot(h_ref[...], wout_ref[...], preferred_element_type=F32)
    if final:
        o_ref[...] = _rms(y, gf_ref[...])
        return
    o_ref[...] = y
    if proj:
        yn = _rms(y, gmix_ref[...]).astype(BF16)
        (_mlstm_in_proj if proj == "mlstm" else _swa_in_proj)(yn, *proj_refs)


def _ffn(x, g, w_in, w_out, g_final, *, layer, tm, final=False, mixer_out=None, attn=None, proj=None):
    n = x.shape[0]
    n_tiles = n // tm
    n_steps = n_tiles
    tile = lambda width: pl.BlockSpec((tm, width), lambda i: (i, 0))
    scratch = [pltpu.VMEM((tm, D_FF), BF16)]
    body_attn = None
    if attn:
        q, kv, sinks, w_mix, mix_layer, tiles_per_seq = attn
        n_steps = n_tiles + 1
        body_attn = (n_tiles, tiles_per_seq)
        prev_blocks = tm // WINDOW
        tile = lambda width: pl.BlockSpec((tm, width), lambda i: (jnp.maximum(i - 1, 0), 0))
        ahead = lambda width: pl.BlockSpec((tm, width), lambda i: (jnp.minimum(i, n_tiles - 1), 0))
        prev = pl.BlockSpec(
            (WINDOW, 2 * A_KV), lambda i: (jnp.maximum(jnp.minimum(i, n_tiles - 1) * prev_blocks - 1, 0), 0))
        scratch += [pltpu.VMEM((2, tm, D_MODEL), BF16), pltpu.VMEM((WINDOW + tm, 2 * A_KV), BF16),
                    pltpu.VMEM((A_KV, WINDOW + tm), BF16)]
    args, specs = [x], [tile(D_MODEL)]
    if attn:
        args += [q, kv, kv, sinks, w_mix]
        specs += [ahead(D_MODEL), prev, ahead(2 * A_KV), pl.BlockSpec(memory_space=pltpu.SMEM),
                  _layer_spec((D_MODEL, D_MODEL), mix_layer)]
    elif mixer_out:
        a, w_mix, mix_layer = mixer_out
        args += [a, w_mix]
        specs += [tile(D_MODEL), _layer_spec((D_MODEL, D_MODEL), mix_layer)]
    args += [g, w_in, w_out, g_final]
    specs += [_layer_spec((1, D_MODEL), layer), _layer_spec((D_MODEL, 2 * D_FF), layer),
              _layer_spec((D_FF, D_MODEL), layer), _const_spec((1, D_MODEL))]
    out_specs = [tile(D_MODEL)]
    out_shape = [jax.ShapeDtypeStruct((n, D_MODEL), F32)]
    kind = None
    if proj:
        kind, gmix, norm_layer = proj[:3]
        args.append(gmix)
        specs.append(_layer_spec((1, D_MODEL), norm_layer))
        if kind == "mlstm":
            w, wg, bg, mix_layer = proj[3:]
            args += [w, wg, bg]
            specs += [_layer_spec((D_MODEL, M_MAIN), mix_layer), _layer_spec((D_MODEL, LANES), mix_layer),
                      _layer_spec((1, LANES), mix_layer)]
            widths = [(M_MAIN, BF16), (LANES, F32)]
        else:
            wq, wkv, cos, sin_signed, mix_layer = proj[3:]
            n_tab = cos.shape[0] // tm
            table = pl.BlockSpec((tm, LANES), lambda i: (i % n_tab, 0))
            args += [wq, wkv, cos, sin_signed]
            specs += [_layer_spec((D_MODEL, D_MODEL), mix_layer), _layer_spec((D_MODEL, 2 * A_KV), mix_layer),
                      table, table]
            widths = [(D_MODEL, BF16), (2 * A_KV, F32)]
        out_specs += [tile(wd) for wd, _ in widths]
        out_shape += [jax.ShapeDtypeStruct((n, wd), dt) for wd, dt in widths]
    return pl.pallas_call(
        functools.partial(_ffn_body, mixer_out=bool(mixer_out), final=final, proj=kind, attn=body_attn),
        grid=(n_steps,),
        in_specs=specs,
        out_specs=out_specs,
        out_shape=out_shape,
        scratch_shapes=scratch,
        compiler_params=_params(1, 56),
        name="half_ffn",
    )(*args)


def _mlstm_body(qkvo_ref, gates_ref, c0_ref, n0_ref, m0_ref, hn_ref,
                hg_ref, c_ref, n_ref, m_ref, *, bg, L):
    @pl.when(pl.program_id(1) == 0)
    def _():
        c_ref[...] = c0_ref[...]
        n_ref[...] = n0_ref[...]
        m_ref[...] = m0_ref[...]

    scale = M_DK ** -0.5
    row = lax.broadcasted_iota(jnp.int32, (L, L), 0)
    col = lax.broadcasted_iota(jnp.int32, (L, L), 1)
    causal = col <= row
    tril = causal.astype(BF16)
    lane = lax.broadcasted_iota(jnp.int32, (L, LANES), 1)

    probs = []
    for b in range(bg):
        gts = gates_ref[b]
        lf = jax.nn.log_sigmoid(gts)
        lf_hi = lf.astype(BF16)
        r1 = lf - lf_hi.astype(F32)
        lf_mid = r1.astype(BF16)
        lf_lo = (r1 - lf_mid.astype(F32)).astype(BF16)
        bcs = (jnp.dot(tril, lf_hi, preferred_element_type=F32)
               + jnp.dot(tril, lf_mid, preferred_element_type=F32)
               + jnp.dot(tril, lf_lo, preferred_element_type=F32))
        both = jnp.where(lane < M_HEADS, gts, bcs)
        if L % LANES:
            both = jnp.concatenate([both, jnp.zeros((LANES - L, LANES), F32)], axis=0)
        both_t = both.T
        for h in range(M_HEADS):
            p = dict(b=b, h=h)
            b_row = both_t[M_HEADS + h:M_HEADS + h + 1, 0:L]
            p["b_last"] = b_row[:, L - 1:L]
            p["bmi_row"] = b_row - both_t[h:h + 1, 0:L]
            b_col = bcs[:, M_HEADS + h:M_HEADS + h + 1]
            p["b_rep"] = jnp.broadcast_to(b_col, (L, LANES))
            p["c_rep"] = jnp.broadcast_to(gts[:, h:h + 1] - b_col, (L, LANES))
            p["m_prev"] = m_ref[b, h:h + 1, 0:1]
            p["n_row"] = n_ref[b, h:h + 1, :]
            p["c_prev"] = c_ref[b, h]
            p["q"] = qkvo_ref[b, :, h * M_DK:(h + 1) * M_DK]
            p["k"] = qkvo_ref[b, :, M_QK + h * M_DK:M_QK + (h + 1) * M_DK]
            p["v"] = qkvo_ref[b, :, 2 * M_QK + h * M_DV:2 * M_QK + (h + 1) * M_DV]
            probs.append(p)

    def wide(c, width):
        return c[:, :width] if width <= LANES else jnp.concatenate([c] * (width // LANES), axis=1)

    def fold(x):
        acc = x[:, :LANES]
        for t in range(1, x.shape[1] // LANES):
            acc = acc + x[:, t * LANES:(t + 1) * LANES]
        return acc

    for p in probs:
        p["s"] = lax.dot_general(p["q"], p["k"], (((1,), (1,)), ((), ())), preferred_element_type=F32)
        p["qc"] = jnp.dot(p["q"], p["c_prev"].astype(BF16), preferred_element_type=F32)
        n_rep = jnp.broadcast_to(p["n_row"], (LANES, M_DK)).astype(BF16)
        p["qn"] = lax.dot_general(p["q"], n_rep, (((1,), (1,)), ((), ())), preferred_element_type=F32)

    for p in probs:
        d = jnp.where(causal, wide(p["b_rep"], L) - p["bmi_row"], -jnp.inf)
        g_rep = p["b_rep"] + p["m_prev"]
        m_t = jnp.maximum(g_rep, jnp.max(d, axis=-1, keepdims=True))
        w = jnp.exp(d - wide(m_t, L)) * (p["s"] * scale)
        p["m_t"] = m_t
        p["inter"] = jnp.exp(g_rep - m_t)
        p["w_sum"] = jnp.sum(fold(w), axis=-1, keepdims=True)
        p["w"] = w.astype(BF16)
        b_last = p["b_last"]
        a_row = b_last - p["bmi_row"]
        m_new = jnp.maximum(p["m_prev"] + b_last, jnp.max(a_row, axis=-1, keepdims=True))
        p["m_new"] = m_new
        p["decay"] = jnp.exp(p["m_prev"] + b_last - m_new)
        p["wk"] = (jnp.exp(b_last + p["c_rep"] - m_new) * scale) * p["k"].astype(F32)

    for p in probs:
        p["wv"] = jnp.dot(p["w"], p["v"], preferred_element_type=F32)
        p["kv"] = lax.dot_general(p["wk"].astype(BF16), p["v"], (((0,), (0,)), ((), ())),
                                  preferred_element_type=F32)

    for p in probs:
        b, h = p["b"], p["h"]
        num = wide(p["inter"], M_DV) * p["qc"] + p["wv"]
        den = p["inter"] * p["qn"] + p["w_sum"]
        hh = num / wide(jnp.maximum(jnp.abs(den), jnp.exp(-p["m_t"])), M_DV)
        hh = hh * lax.rsqrt(jnp.mean(hh * hh, axis=-1, keepdims=True) + EPS)
        hh = hh * hn_ref[:, h * M_DV:(h + 1) * M_DV]
        og = qkvo_ref[b, :, 2 * M_QK + M_V + h * M_DV:2 * M_QK + M_V + (h + 1) * M_DV]
        hg_ref[b, :, h * M_DV:(h + 1) * M_DV] = (jax.nn.sigmoid(og.astype(F32)) * hh).astype(BF16)
        c_ref[b, h] = p["decay"] * p["c_prev"] + p["kv"]
        n_ref[b, h:h + 1, :] = p["decay"] * p["n_row"] + jnp.sum(p["wk"], axis=0, keepdims=True)
        m_ref[b, h:h + 1, :] = jnp.broadcast_to(p["m_new"], (1, LANES))


def _mlstm(qkvo, gates, c0, n0, m0, hnorm, *, layer, state_layer, bg, L):
    nb, t, _ = qkvo.shape
    state_shapes = [(bg, M_HEADS, M_DK, M_DV), (bg, M_HEADS, M_DK), (bg, M_HEADS, LANES)]
    state_specs = [pl.BlockSpec(shp, lambda g, c, nd=len(shp): (g,) + (0,) * (nd - 1))
                   for shp in state_shapes]
    if state_layer is None:
        init_specs = state_specs
    else:
        init_specs = [pl.BlockSpec((None, *shp), lambda g, c, nd=len(shp): (state_layer, g) + (0,) * (nd - 1))
                      for shp in state_shapes]
    return pl.pallas_call(
        functools.partial(_mlstm_body, bg=bg, L=L),
        grid=(nb // bg, t // L),
        in_specs=[
            pl.BlockSpec((bg, L, M_MAIN), lambda g, c: (g, c, 0)),
            pl.BlockSpec((bg, L, LANES), lambda g, c: (g, c, 0)),
            *init_specs,
            _layer_spec((1, M_V), layer),
        ],
        out_specs=[pl.BlockSpec((bg, L, M_V), lambda g, c: (g, c, 0)), *state_specs],
        out_shape=[
            jax.ShapeDtypeStruct((nb, t, M_V), BF16),
            jax.ShapeDtypeStruct((nb, M_HEADS, M_DK, M_DV), F32),
            jax.ShapeDtypeStruct((nb, M_HEADS, M_DK), F32),
            jax.ShapeDtypeStruct((nb, M_HEADS, LANES), F32),
        ],
        compiler_params=_params(2, 32),
        name="mlstm_scan",
    )(qkvo, gates, c0, n0, m0, hnorm)


def _attn_groups(sinks_ref, q_ref, o_ref, kv_scr, vt_scr, problems):
    scores = []
    for q_lo, q_rows, k_lo, k_rows, g, first_valid in problems:
        kb = kv_scr[k_lo:k_lo + k_rows, g * A_HD:(g + 1) * A_HD]
        qs = jnp.concatenate(
            [q_ref[q_lo:q_lo + q_rows, hd * A_HD:(hd + 1) * A_HD]
             for hd in range(A_GROUP * g, A_GROUP * (g + 1))], axis=0)
        scores.append(lax.dot_general(kb, qs, (((1,), (1,)), ((), ())), preferred_element_type=F32))
    probs = []
    for s, (q_lo, q_rows, k_lo, k_rows, g, first_valid) in zip(scores, problems):
        if first_valid is not None:
            key = lax.broadcasted_iota(jnp.int32, s.shape, 0)
            s = jnp.where(key >= first_valid, s, -jnp.inf)
        sk = jnp.concatenate([jnp.full((1, q_rows), sinks_ref[hd], F32)
                              for hd in range(A_GROUP * g, A_GROUP * (g + 1))], axis=1)
        m = jnp.maximum(jnp.max(s, axis=0, keepdims=True), sk)
        p = jnp.exp(s - m)
        den = jnp.sum(p, axis=0, keepdims=True) + jnp.exp(sk - m)
        probs.append((p.astype(BF16), den))
    outs = []
    for (p, den), (q_lo, q_rows, k_lo, k_rows, g, first_valid) in zip(probs, problems):
        if vt_scr is None:
            vb = kv_scr[k_lo:k_lo + k_rows, A_KV + g * A_HD:A_KV + (g + 1) * A_HD]
            o_t = lax.dot_general(vb, p, (((0,), (0,)), ((), ())), preferred_element_type=F32)
        else:
            o_t = jnp.dot(vt_scr[g * A_HD:(g + 1) * A_HD, k_lo:k_lo + k_rows], p,
                          preferred_element_type=F32)
        outs.append(o_t / den)
    for o_t, (q_lo, q_rows, k_lo, k_rows, g, first_valid) in zip(outs, problems):
        o = o_t.T
        for j in range(A_GROUP):
            hd = A_GROUP * g + j
            o_ref[q_lo:q_lo + q_rows, hd * A_HD:(hd + 1) * A_HD] = (
                o[j * q_rows:(j + 1) * q_rows].astype(BF16))


def _attn_prompt_tile(sinks_ref, q_ref, kvp_ref, kvo_ref, o_ref, kv_scr, vt_scr, first_of_seq):
    rows = q_ref.shape[0]
    kv_scr[0:WINDOW, :] = kvp_ref[...].astype(BF16)
    kv_scr[WINDOW:WINDOW + rows, :] = kvo_ref[...].astype(BF16)
    vt_scr[:, 0:WINDOW] = kvp_ref[:, A_KV:2 * A_KV].T.astype(BF16)
    vt_scr[:, WINDOW:WINDOW + rows] = kvo_ref[:, A_KV:2 * A_KV].T.astype(BF16)
    band = WINDOW + CHUNK
    n_missing = jnp.where(first_of_seq, WINDOW, 0)
    problems = []
    for i in range(rows // CHUNK):
        first_valid = (n_missing - i * CHUNK) if i * CHUNK < WINDOW else None
        for g in range(A_KV_HEADS):
            problems.append((i * CHUNK, CHUNK, i * CHUNK, band, g, first_valid))
    _attn_groups(sinks_ref, q_ref, o_ref, kv_scr, vt_scr, problems)


def _attn_sample_body(sinks_ref, q_ref, ck_ref, cv_ref, kv_ref, o_ref, kv_scr):
    rows = ck_ref.shape[1]
    t = q_ref.shape[1]
    kv_scr[0:rows, 0:A_KV] = ck_ref[0].astype(BF16)
    kv_scr[0:rows, A_KV:2 * A_KV] = cv_ref[0].astype(BF16)
    kv_scr[rows:rows + t, :] = kv_ref[0].astype(BF16)
    _attn_groups(sinks_ref, q_ref.at[0], o_ref.at[0], kv_scr, None,
                 [(0, t, 0, rows + t, g, None) for g in range(A_KV_HEADS)])


def _attn_sample(q, cache_k, cache_v, kv, sinks, *, layer):
    nb, t, _ = q.shape
    rows = cache_k.shape[2]
    return pl.pallas_call(
        _attn_sample_body,
        grid=(nb,),
        in_specs=[
            pl.BlockSpec(memory_space=pltpu.SMEM),
            pl.BlockSpec((1, t, D_MODEL), lambda b: (b, 0, 0)),
            pl.BlockSpec((None, 1, rows, A_KV), lambda b: (layer, b, 0, 0)),
            pl.BlockSpec((None, 1, rows, A_KV), lambda b: (layer, b, 0, 0)),
            pl.BlockSpec((1, t, 2 * A_KV), lambda b: (b, 0, 0)),
        ],
        out_specs=pl.BlockSpec((1, t, D_MODEL), lambda b: (b, 0, 0)),
        out_shape=jax.ShapeDtypeStruct((nb, t, D_MODEL), BF16),
        scratch_shapes=[pltpu.VMEM((rows + t, 2 * A_KV), BF16)],
        compiler_params=_params(1, 32),
        name="swa_sample_attn",
    )(sinks, q, cache_k, cache_v, kv)


def _rope_tables(pos):
    inv = ROPE_THETA ** (-jnp.arange(0, A_HD, 2, dtype=F32) / A_HD)
    ang = pos.astype(F32)[:, None] * inv[None, :]
    cos = jnp.cos(ang)
    sin = jnp.sin(ang)
    reps = LANES // A_HD
    return (jnp.concatenate([cos, cos] * reps, axis=-1),
            jnp.concatenate([-sin, sin] * reps, axis=-1))


def kernel(x_prompt, x_sample, state_mlstm_C, state_mlstm_n, state_mlstm_m, cache_swa_k, cache_swa_v,
           ffn_norm1, ffn_w_in1, ffn_w_out1, mix_norm, mlstm_w_in, mlstm_b_gates, mlstm_head_norm,
           mlstm_w_out, swa_w_qkv, swa_sinks, swa_w_out, ffn_norm2, ffn_w_in2, ffn_w_out2, final_norm):
    bp, tp, _ = x_prompt.shape
    bs, ts, _ = x_sample.shape
    tm_p = 512
    tm_s = bs * ts

    w_in1, w_out1 = ffn_w_in1.astype(BF16), ffn_w_out1.astype(BF16)
    w_in2, w_out2 = ffn_w_in2.astype(BF16), ffn_w_out2.astype(BF16)
    m_w_main = mlstm_w_in[:, :, :M_MAIN].astype(BF16)
    n_gates = 2 * M_HEADS
    m_w_gates = jnp.pad(mlstm_w_in[:, :, M_MAIN:], ((0, 0), (0, 0), (0, LANES - n_gates))).astype(BF16)
    m_b_gates = jnp.pad(mlstm_b_gates.astype(F32), ((0, 0), (0, LANES - n_gates)))[:, None, :]
    m_w_out = mlstm_w_out.astype(BF16)
    s_wq = swa_w_qkv[:, :, :D_MODEL].astype(BF16)
    s_wkv = swa_w_qkv[:, :, D_MODEL:].astype(BF16)
    s_w_out = swa_w_out.astype(BF16)
    sinks = swa_sinks.astype(F32)
    rows3 = lambda a: a.astype(F32)[:, None, :]
    g1, g2, gmix, hnorm = rows3(ffn_norm1), rows3(ffn_norm2), rows3(mix_norm), rows3(mlstm_head_norm)
    gf = final_norm.astype(F32)[None, :]

    cos_p, sin_p = _rope_tables(jnp.arange(tp))
    cos_s, sin_s = _rope_tables(PAST_LEN + jnp.arange(ts))
    cos_s, sin_s = jnp.tile(cos_s, (bs, 1)), jnp.tile(sin_s, (bs, 1))

    yp = x_prompt.reshape(bp * tp, D_MODEL)
    ys = x_sample.reshape(bs * ts, D_MODEL)
    zc = jnp.zeros((bp, M_HEADS, M_DK, M_DV), F32)
    zn = jnp.zeros((bp, M_HEADS, M_DK), F32)
    zm = jnp.zeros((bp, M_HEADS, LANES), F32)
    c0_s = state_mlstm_C.astype(F32)
    n0_s = state_mlstm_n.astype(F32)
    m0_s = jnp.broadcast_to(state_mlstm_m.astype(F32)[..., None], state_mlstm_m.shape + (LANES,))
    rows = cache_swa_k.shape[2]
    ck = cache_swa_k.astype(F32).reshape(-1, bs, rows, A_KV)
    cv = cache_swa_v.astype(F32).reshape(-1, bs, rows, A_KV)

    p_c, p_n, p_m, p_k, p_v = [], [], [], [], []
    s_c, s_n, s_m, s_k, s_v = [], [], [], [], []
    for i in range(DEPTH):
        j = i // 2
        if i % 2 == 0:
            proj_p = proj_s = ("mlstm", gmix, i, m_w_main, m_w_gates, m_b_gates, j)
        else:
            proj_p = ("swa", gmix, i, s_wq, s_wkv, cos_p, sin_p, j)
            proj_s = ("swa", gmix, i, s_wq, s_wkv, cos_s, sin_s, j)
        yp, pa_p, pb_p = _ffn(yp, g1, w_in1, w_out1, gf, layer=i, tm=tm_p, proj=proj_p)
        ys, pa_s, pb_s = _ffn(ys, g1, w_in1, w_out1, gf, layer=i, tm=tm_s, proj=proj_s)
        if i % 2 == 0:
            qkvo, gates = pa_p, pb_p
            hg, c, n, m = _mlstm(qkvo.reshape(bp, tp, M_MAIN), gates.reshape(bp, tp, LANES),
                                 zc, zn, zm, hnorm, layer=j, state_layer=None, bg=4,
                                 L=min(M_SCAN_CHUNK, tp))
            mix_p = dict(mixer_out=(hg.reshape(bp * tp, M_V), m_w_out, j))
            p_c.append(c); p_n.append(n); p_m.append(m[:, :, 0])

            qkvo, gates = pa_s, pb_s
            hg, c, n, m = _mlstm(qkvo.reshape(bs, ts, M_MAIN), gates.reshape(bs, ts, LANES),
                                 c0_s, n0_s, m0_s, hnorm, layer=j, state_layer=j, bg=bs,
                                 L=min(CHUNK, ts))
            mix_s = (hg.reshape(bs * ts, M_V), m_w_out, j)
            s_c.append(c); s_n.append(n); s_m.append(m[:, :, 0])
        else:
            q, kv = pa_p, pb_p
            kv3 = kv.reshape(bp, tp, 2 * A_KV)
            mix_p = dict(attn=(q, kv, sinks[j], s_w_out, j, tp // tm_p))
            keep = min(WINDOW, tp)
            p_k.append(kv3[:, tp - keep:, :A_KV].reshape(bp, keep, A_KV_HEADS, A_HD))
            p_v.append(kv3[:, tp - keep:, A_KV:].reshape(bp, keep, A_KV_HEADS, A_HD))

            q, kv = pa_s, pb_s
            kv3 = kv.reshape(bs, ts, 2 * A_KV)
            o = _attn_sample(q.reshape(bs, ts, D_MODEL), ck, cv, kv3, sinks[j], layer=j)
            mix_s = (o.reshape(bs * ts, D_MODEL), s_w_out, j)
            s_k.append(kv3[:, :, :A_KV].reshape(bs, ts, A_KV_HEADS, A_HD))
            s_v.append(kv3[:, :, A_KV:].reshape(bs, ts, A_KV_HEADS, A_HD))
        last = i == DEPTH - 1
        yp, = _ffn(yp, g2, w_in2, w_out2, gf, layer=i, tm=tm_p, final=last, **mix_p)
        ys, = _ffn(ys, g2, w_in2, w_out2, gf, layer=i, tm=tm_s, final=last, mixer_out=mix_s)

    return (yp.reshape(bp, tp, D_MODEL), ys.reshape(bs, ts, D_MODEL),
            jnp.stack(p_c), jnp.stack(p_n), jnp.stack(p_m), jnp.stack(p_k), jnp.stack(p_v),
            jnp.stack(s_c), jnp.stack(s_n), jnp.stack(s_m), jnp.stack(s_k), jnp.stack(s_v))
```

```python
import functools
import itertools

import jax
import jax.numpy as jnp
from jax import lax
from jax.experimental import pallas as pl
from jax.experimental.pallas import tpu as pltpu

F32 = jnp.float32
BF16 = jnp.bfloat16

D_MODEL = 1024
DEPTH = 4
CHUNK = 64
M_HEADS = 4
M_DK = D_MODEL // 8
M_DV = D_MODEL // M_HEADS
M_QK = M_HEADS * M_DK
M_V = M_HEADS * M_DV
M_MAIN = 2 * M_QK + 2 * M_V
A_HEADS = 16
A_KV_HEADS = 4
A_HD = D_MODEL // A_HEADS
A_GROUP = A_HEADS // A_KV_HEADS
A_KV = A_KV_HEADS * A_HD
WINDOW = 128
PAST_LEN = 4096
ROPE_THETA = 10000.0
D_FF = 11 * D_MODEL // 4
FFN_RES = 0.5
EPS = 1e-6

LANES = 128
FFN_CHUNK = 256
PROJ_CHUNK = 512
M_SCAN_CHUNK = 256
MIB = 1024 * 1024


def _rms(x, g):
    return x * lax.rsqrt(jnp.mean(x * x, axis=-1, keepdims=True) + EPS) * g


def _const_spec(shape):
    nd = len(shape)
    return pl.BlockSpec(shape, lambda *_: (0,) * nd, pipeline_mode=pl.Buffered(1))


def _layer_spec(shape, layer):
    nd = len(shape)
    return pl.BlockSpec((None, *shape), lambda *_: (layer,) + (0,) * nd, pipeline_mode=pl.Buffered(1))


def _run(work):
    for _ in work:
        pass


def _params(n_grid, vmem_mib):
    return pltpu.CompilerParams(
        dimension_semantics=("arbitrary",) * n_grid,
        vmem_limit_bytes=vmem_mib * MIB)


def _mlstm_in_proj(xn, w_ref, wg_ref, bg_ref, o_ref, og_ref):
    for c in range(M_MAIN // PROJ_CHUNK):
        lo = c * PROJ_CHUNK
        o_ref[:, lo:lo + PROJ_CHUNK] = jnp.dot(
            xn, w_ref[:, lo:lo + PROJ_CHUNK], preferred_element_type=F32).astype(BF16)
    og_ref[...] = jnp.dot(xn, wg_ref[...], preferred_element_type=F32) + bg_ref[...]


def _rope(x, cos, sin_signed, first_half):
    swapped = jnp.where(first_half, pltpu.roll(x, LANES - A_HD // 2, 1), pltpu.roll(x, A_HD // 2, 1))
    return x * cos + swapped * sin_signed


def _swa_in_proj(xn, wq_ref, wkv_ref, cos_ref, sin_ref, q_ref, kv_ref):
    cos = cos_ref[...]
    sin_signed = sin_ref[...]
    lane = lax.broadcasted_iota(jnp.int32, cos.shape, 1)
    first_half = (lane & (A_HD // 2)) == 0
    q_scale = A_HD ** -0.5
    for c in range(D_MODEL // PROJ_CHUNK):
        q = jnp.dot(xn, wq_ref[:, c * PROJ_CHUNK:(c + 1) * PROJ_CHUNK], preferred_element_type=F32)
        for j in range(PROJ_CHUNK // LANES):
            lo = c * PROJ_CHUNK + j * LANES
            blk = _rope(q[:, j * LANES:(j + 1) * LANES], cos, sin_signed, first_half)
            q_ref[:, lo:lo + LANES] = (blk * q_scale).astype(BF16)
    kv = jnp.dot(xn, wkv_ref[...], preferred_element_type=F32)
    for j in range(A_KV // LANES):
        kv_ref[:, j * LANES:(j + 1) * LANES] = _rope(
            kv[:, j * LANES:(j + 1) * LANES], cos, sin_signed, first_half)
    kv_ref[:, A_KV:2 * A_KV] = kv[:, A_KV:2 * A_KV]


def _ffn_body(*refs, mixer_out, final, proj, attn, scan):
    refs = iter(refs)
    x_ref = next(refs)
    if attn:
        q_ref, kvp_ref, kvo_ref, sinks_ref, wmix_ref = [next(refs) for _ in range(5)]
    elif scan:
        qkvo_ref, gates_ref, hn_ref, wmix_ref = [next(refs) for _ in range(4)]
    elif mixer_out:
        a_ref, wmix_ref = next(refs), next(refs)
    g_ref, win_ref, wout_ref, gf_ref = next(refs), next(refs), next(refs), next(refs)
    if proj:
        gmix_ref = next(refs)
        proj_refs = [next(refs) for _ in range(3 if proj == "mlstm" else 4)]
    o_ref = next(refs)
    if proj:
        proj_refs += [next(refs), next(refs)]
    if scan:
        c_ref, n_ref, m_ref = next(refs), next(refs), next(refs)
    h_ref = next(refs)

    if attn or scan:
        n_tiles, tiles_per_seq = attn or scan
        a_scr = next(refs)
        s = pl.program_id(0)

        @pl.when(s == 0)
        def _():
            a_scr[...] = jnp.zeros_like(a_scr)

        slot = lax.rem(s, 2)
        a = a_scr[1 - slot]
        x = x_ref[...] + jnp.dot(a, wmix_ref[...], preferred_element_type=F32)
        first_of_seq = lax.rem(jnp.minimum(s, n_tiles - 1), tiles_per_seq) == 0
        if attn:
            kv_scr, vt_scr = next(refs), next(refs)
            mixer_work = _attn_prompt_tile(sinks_ref, q_ref, kvp_ref, kvo_ref, a_scr.at[slot], kv_scr,
                                           vt_scr, first_of_seq)
            mixer_pace = 2
        else:
            live = s < n_tiles

            @pl.when(first_of_seq & live)
            def _():
                c_ref[...] = jnp.zeros_like(c_ref)
                n_ref[...] = jnp.zeros_like(n_ref)
                m_ref[...] = jnp.zeros_like(m_ref)

            L = M_SCAN_CHUNK
            mixer_work = itertools.chain.from_iterable(
                _mlstm_chunk([dict(qkvo=qkvo_ref.at[pl.ds(lo, L)], gates=gates_ref.at[pl.ds(lo, L)],
                                   hg=a_scr.at[slot, pl.ds(lo, L)],
                                   c=c_ref.at[0], n=n_ref.at[0], m=m_ref.at[0])],
                             hn_ref, L, keep=live)
                for lo in range(0, x_ref.shape[0], L))
            mixer_pace = 1
    else:
        mixer_work, mixer_pace = iter(()), 0
        if mixer_out:
            x = x_ref[...] + jnp.dot(a_ref[...], wmix_ref[...], preferred_element_type=F32)
        else:
            x = x_ref[...]

    xn = _rms(x, g_ref[...]).astype(BF16)
    for c in range(D_FF // FFN_CHUNK):
        lo = c * FFN_CHUNK
        gate = jnp.dot(xn, win_ref[:, lo:lo + FFN_CHUNK], preferred_element_type=F32)
        up = jnp.dot(xn, win_ref[:, D_FF + lo:D_FF + lo + FFN_CHUNK], preferred_element_type=F32)
        h_ref[:, lo:lo + FFN_CHUNK] = (gate * jax.nn.sigmoid(gate) * up).astype(BF16)
        for _ in range(mixer_pace):
            next(mixer_work, None)
    _run(mixer_work)
    y = x + FFN_RES * jnp.dot(h_ref[...], wout_ref[...], preferred_element_type=F32)
    if final:
        o_ref[...] = _rms(y, gf_ref[...])
        return
    o_ref[...] = y
    if proj:
        yn = _rms(y, gmix_ref[...]).astype(BF16)
        (_mlstm_in_proj if proj == "mlstm" else _swa_in_proj)(yn, *proj_refs)


def _ffn(x, g, w_in, w_out, g_final, *, layer, tm, final=False, mixer_out=None, attn=None, scan=None,
         proj=None):
    n = x.shape[0]
    n_tiles = n // tm
    n_steps = n_tiles
    tile = lambda width: pl.BlockSpec((tm, width), lambda i: (i, 0))
    scratch = [pltpu.VMEM((tm, D_FF), BF16)]
    body_attn = body_scan = None
    if attn or scan:
        n_steps = n_tiles + 1
        tile = lambda width: pl.BlockSpec((tm, width), lambda i: (jnp.maximum(i - 1, 0), 0))
        ahead = lambda width: pl.BlockSpec((tm, width), lambda i: (jnp.minimum(i, n_tiles - 1), 0))
        scratch.append(pltpu.VMEM((2, tm, D_MODEL), BF16))
    args, specs = [x], [tile(D_MODEL)]
    if attn:
        q, kv, sinks, w_mix, mix_layer, tiles_per_seq = attn
        body_attn = (n_tiles, tiles_per_seq)
        prev_blocks = tm // WINDOW
        prev = pl.BlockSpec(
            (WINDOW, 2 * A_KV), lambda i: (jnp.maximum(jnp.minimum(i, n_tiles - 1) * prev_blocks - 1, 0), 0))
        scratch += [pltpu.VMEM((WINDOW + tm, 2 * A_KV), BF16), pltpu.VMEM((A_KV, WINDOW + tm), BF16)]
        args += [q, kv, kv, sinks, w_mix]
        specs += [ahead(D_MODEL), prev, ahead(2 * A_KV), pl.BlockSpec(memory_space=pltpu.SMEM),
                  _layer_spec((D_MODEL, D_MODEL), mix_layer)]
    elif scan:
        qkvo, gates, hnorm, w_mix, mix_layer, tiles_per_seq = scan
        body_scan = (n_tiles, tiles_per_seq)
        args += [qkvo, gates, hnorm, w_mix]
        specs += [ahead(M_MAIN), ahead(LANES), _layer_spec((1, M_V), mix_layer),
                  _layer_spec((D_MODEL, D_MODEL), mix_layer)]
    elif mixer_out:
        a, w_mix, mix_layer = mixer_out
        args += [a, w_mix]
        specs += [tile(D_MODEL), _layer_spec((D_MODEL, D_MODEL), mix_layer)]
    args += [g, w_in, w_out, g_final]
    specs += [_layer_spec((1, D_MODEL), layer), _layer_spec((D_MODEL, 2 * D_FF), layer),
              _layer_spec((D_FF, D_MODEL), layer), _const_spec((1, D_MODEL))]
    out_specs = [tile(D_MODEL)]
    out_shape = [jax.ShapeDtypeStruct((n, D_MODEL), F32)]
    kind = None
    if proj:
        kind, gmix, norm_layer = proj[:3]
        args.append(gmix)
        specs.append(_layer_spec((1, D_MODEL), norm_layer))
        if kind == "mlstm":
            w, wg, bg, mix_layer = proj[3:]
            args += [w, wg, bg]
            specs += [_layer_spec((D_MODEL, M_MAIN), mix_layer), _layer_spec((D_MODEL, LANES), mix_layer),
                      _layer_spec((1, LANES), mix_layer)]
            widths = [(M_MAIN, BF16), (LANES, F32)]
        else:
            wq, wkv, cos, sin_signed, mix_layer = proj[3:]
            n_tab = cos.shape[0] // tm
            table = pl.BlockSpec((tm, LANES), lambda i: (i % n_tab, 0))
            args += [wq, wkv, cos, sin_signed]
            specs += [_layer_spec((D_MODEL, D_MODEL), mix_layer), _layer_spec((D_MODEL, 2 * A_KV), mix_layer),
                      table, table]
            widths = [(D_MODEL, BF16), (2 * A_KV, F32)]
        out_specs += [tile(wd) for wd, _ in widths]
        out_shape += [jax.ShapeDtypeStruct((n, wd), dt) for wd, dt in widths]
    if scan:
        n_seqs = n_tiles // tiles_per_seq
        for shp in [(M_HEADS, M_DK, M_DV), (M_HEADS, M_DK), (M_HEADS, LANES)]:
            out_specs.append(pl.BlockSpec(
                (1, *shp),
                lambda i, nd=len(shp): (jnp.minimum(i, n_tiles - 1) // tiles_per_seq,) + (0,) * nd))
            out_shape.append(jax.ShapeDtypeStruct((n_seqs, *shp), F32))
    return pl.pallas_call(
        functools.partial(_ffn_body, mixer_out=bool(mixer_out), final=final, proj=kind, attn=body_attn,
                          scan=body_scan),
        grid=(n_steps,),
        in_specs=specs,
        out_specs=out_specs,
        out_shape=out_shape,
        scratch_shapes=scratch,
        compiler_params=_params(1, 56),
        name="half_ffn",
    )(*args)


def _mlstm_body(qkvo_ref, gates_ref, c0_ref, n0_ref, m0_ref, hn_ref,
                hg_ref, c_ref, n_ref, m_ref, *, bg, L):
    @pl.when(pl.program_id(1) == 0)
    def _():
        c_ref[...] = c0_ref[...]
        n_ref[...] = n0_ref[...]
        m_ref[...] = m0_ref[...]

    _run(_mlstm_chunk([dict(qkvo=qkvo_ref.at[b], gates=gates_ref.at[b], hg=hg_ref.at[b],
                            c=c_ref.at[b], n=n_ref.at[b], m=m_ref.at[b]) for b in range(bg)],
                      hn_ref, L))


def _mlstm_chunk(seqs, hn_ref, L, keep=None):
    scale = M_DK ** -0.5
    row = lax.broadcasted_iota(jnp.int32, (L, L), 0)
    col = lax.broadcasted_iota(jnp.int32, (L, L), 1)
    causal = col <= row
    tril = causal.astype(BF16)
    lane = lax.broadcasted_iota(jnp.int32, (L, LANES), 1)

    probs = []
    for seq in seqs:
        gts = seq["gates"][...]
        lf = jax.nn.log_sigmoid(gts)
        lf_hi = lf.astype(BF16)
        r1 = lf - lf_hi.astype(F32)
        lf_mid = r1.astype(BF16)
        lf_lo = (r1 - lf_mid.astype(F32)).astype(BF16)
        bcs = (jnp.dot(tril, lf_hi, preferred_element_type=F32)
               + jnp.dot(tril, lf_mid, preferred_element_type=F32)
               + jnp.dot(tril, lf_lo, preferred_element_type=F32))
        both = jnp.where(lane < M_HEADS, gts, bcs)
        if L % LANES:
            both = jnp.concatenate([both, jnp.zeros((LANES - L, LANES), F32)], axis=0)
        both_t = both.T
        for h in range(M_HEADS):
            p = dict(seq=seq, h=h)
            b_row = both_t[M_HEADS + h:M_HEADS + h + 1, 0:L]
            p["b_last"] = b_row[:, L - 1:L]
            p["bmi_row"] = b_row - both_t[h:h + 1, 0:L]
            b_col = bcs[:, M_HEADS + h:M_HEADS + h + 1]
            p["b_rep"] = jnp.broadcast_to(b_col, (L, LANES))
            p["c_rep"] = jnp.broadcast_to(gts[:, h:h + 1] - b_col, (L, LANES))
            qkvo_ref = seq["qkvo"]
            p["m_prev"] = seq["m"][h:h + 1, 0:1]
            p["n_row"] = seq["n"][h:h + 1, :]
            p["c_prev"] = seq["c"][h]
            p["q"] = qkvo_ref[:, h * M_DK:(h + 1) * M_DK]
            p["k"] = qkvo_ref[:, M_QK + h * M_DK:M_QK + (h + 1) * M_DK]
            p["v"] = qkvo_ref[:, 2 * M_QK + h * M_DV:2 * M_QK + (h + 1) * M_DV]
            probs.append(p)

    def wide(c, width):
        return c[:, :width] if width <= LANES else jnp.concatenate([c] * (width // LANES), axis=1)

    def fold(x):
        acc = x[:, :LANES]
        for t in range(1, x.shape[1] // LANES):
            acc = acc + x[:, t * LANES:(t + 1) * LANES]
        return acc

    yield
    for p in probs:
        p["s"] = lax.dot_general(p["q"], p["k"], (((1,), (1,)), ((), ())), preferred_element_type=F32)
        p["qc"] = jnp.dot(p["q"], p["c_prev"].astype(BF16), preferred_element_type=F32)
        n_rep = jnp.broadcast_to(p["n_row"], (LANES, M_DK)).astype(BF16)
        p["qn"] = lax.dot_general(p["q"], n_rep, (((1,), (1,)), ((), ())), preferred_element_type=F32)

    yield
    for p in probs:
        d = jnp.where(causal, wide(p["b_rep"], L) - p["bmi_row"], -jnp.inf)
        g_rep = p["b_rep"] + p["m_prev"]
        m_t = jnp.maximum(g_rep, jnp.max(d, axis=-1, keepdims=True))
        w = jnp.exp(d - wide(m_t, L)) * (p["s"] * scale)
        p["m_t"] = m_t
        p["inter"] = jnp.exp(g_rep - m_t)
        p["w_sum"] = jnp.sum(fold(w), axis=-1, keepdims=True)
        p["w"] = w.astype(BF16)
        b_last = p["b_last"]
        a_row = b_last - p["bmi_row"]
        m_new = jnp.maximum(p["m_prev"] + b_last, jnp.max(a_row, axis=-1, keepdims=True))
        p["m_new"] = m_new
        p["decay"] = jnp.exp(p["m_prev"] + b_last - m_new)
        p["wk"] = (jnp.exp(b_last + p["c_rep"] - m_new) * scale) * p["k"].astype(F32)

    yield
    for p in probs:
        p["wv"] = jnp.dot(p["w"], p["v"], preferred_element_type=F32)
        p["kv"] = lax.dot_general(p["wk"].astype(BF16), p["v"], (((0,), (0,)), ((), ())),
                                  preferred_element_type=F32)

    def updated(new, old):
        return new if keep is None else jnp.where(keep, new, old)

    yield
    for p in probs:
        seq, h = p["seq"], p["h"]
        num = wide(p["inter"], M_DV) * p["qc"] + p["wv"]
        den = p["inter"] * p["qn"] + p["w_sum"]
        hh = num / wide(jnp.maximum(jnp.abs(den), jnp.exp(-p["m_t"])), M_DV)
        hh = hh * lax.rsqrt(jnp.mean(hh * hh, axis=-1, keepdims=True) + EPS)
        hh = hh * hn_ref[:, h * M_DV:(h + 1) * M_DV]
        og = seq["qkvo"][:, 2 * M_QK + M_V + h * M_DV:2 * M_QK + M_V + (h + 1) * M_DV]
        seq["hg"][:, h * M_DV:(h + 1) * M_DV] = (jax.nn.sigmoid(og.astype(F32)) * hh).astype(BF16)
        seq["c"][h] = updated(p["decay"] * p["c_prev"] + p["kv"], p["c_prev"])
        seq["n"][h:h + 1, :] = updated(
            p["decay"] * p["n_row"] + jnp.sum(p["wk"], axis=0, keepdims=True), p["n_row"])
        m_row = jnp.broadcast_to(p["m_new"], (1, LANES))
        seq["m"][h:h + 1, :] = updated(m_row, jnp.broadcast_to(p["m_prev"], (1, LANES)))


def _mlstm(qkvo, gates, c0, n0, m0, hnorm, *, layer, bg, L):
    nb, t, _ = qkvo.shape
    state_shapes = [(bg, M_HEADS, M_DK, M_DV), (bg, M_HEADS, M_DK), (bg, M_HEADS, LANES)]
    state_specs = [pl.BlockSpec(shp, lambda g, c, nd=len(shp): (g,) + (0,) * (nd - 1))
                   for shp in state_shapes]
    init_specs = [pl.BlockSpec((None, *shp), lambda g, c, nd=len(shp): (layer, g) + (0,) * (nd - 1))
                  for shp in state_shapes]
    return pl.pallas_call(
        functools.partial(_mlstm_body, bg=bg, L=L),
        grid=(nb // bg, t // L),
        in_specs=[
            pl.BlockSpec((bg, L, M_MAIN), lambda g, c: (g, c, 0)),
            pl.BlockSpec((bg, L, LANES), lambda g, c: (g, c, 0)),
            *init_specs,
            _layer_spec((1, M_V), layer),
        ],
        out_specs=[pl.BlockSpec((bg, L, M_V), lambda g, c: (g, c, 0)), *state_specs],
        out_shape=[
            jax.ShapeDtypeStruct((nb, t, M_V), BF16),
            jax.ShapeDtypeStruct((nb, M_HEADS, M_DK, M_DV), F32),
            jax.ShapeDtypeStruct((nb, M_HEADS, M_DK), F32),
            jax.ShapeDtypeStruct((nb, M_HEADS, LANES), F32),
        ],
        compiler_params=_params(2, 32),
        name="mlstm_scan",
    )(qkvo, gates, c0, n0, m0, hnorm)


def _attn_groups(sinks_ref, q_ref, o_ref, kv_scr, vt_scr, problems):
    per_slice = 8

    def pace(idx):
        return (idx + 1) % per_slice == 0

    scores = []
    for idx, (q_lo, q_rows, k_lo, k_rows, g, first_valid) in enumerate(problems):
        kb = kv_scr[k_lo:k_lo + k_rows, g * A_HD:(g + 1) * A_HD]
        qs = jnp.concatenate(
            [q_ref[q_lo:q_lo + q_rows, hd * A_HD:(hd + 1) * A_HD]
             for hd in range(A_GROUP * g, A_GROUP * (g + 1))], axis=0)
        scores.append(lax.dot_general(kb, qs, (((1,), (1,)), ((), ())), preferred_element_type=F32))
        if pace(idx):
            yield
    probs = []
    for idx, (s, (q_lo, q_rows, k_lo, k_rows, g, first_valid)) in enumerate(zip(scores, problems)):
        if first_valid is not None:
            key = lax.broadcasted_iota(jnp.int32, s.shape, 0)
            s = jnp.where(key >= first_valid, s, -jnp.inf)
        sk = jnp.concatenate([jnp.full((1, q_rows), sinks_ref[hd], F32)
                              for hd in range(A_GROUP * g, A_GROUP * (g + 1))], axis=1)
        m = jnp.maximum(jnp.max(s, axis=0, keepdims=True), sk)
        p = jnp.exp(s - m)
        den = jnp.sum(p, axis=0, keepdims=True) + jnp.exp(sk - m)
        probs.append((p.astype(BF16), den))
        if pace(idx):
            yield
    outs = []
    for idx, ((p, den), (q_lo, q_rows, k_lo, k_rows, g, first_valid)) in enumerate(zip(probs, problems)):
        if vt_scr is None:
            vb = kv_scr[k_lo:k_lo + k_rows, A_KV + g * A_HD:A_KV + (g + 1) * A_HD]
            o_t = lax.dot_general(vb, p, (((0,), (0,)), ((), ())), preferred_element_type=F32)
        else:
            o_t = jnp.dot(vt_scr[g * A_HD:(g + 1) * A_HD, k_lo:k_lo + k_rows], p,
                          preferred_element_type=F32)
        outs.append(o_t / den)
        if pace(idx):
            yield
    for idx, (o_t, (q_lo, q_rows, k_lo, k_rows, g, first_valid)) in enumerate(zip(outs, problems)):
        o = o_t.T
        for j in range(A_GROUP):
            hd = A_GROUP * g + j
            o_ref[q_lo:q_lo + q_rows, hd * A_HD:(hd + 1) * A_HD] = (
                o[j * q_rows:(j + 1) * q_rows].astype(BF16))
        if pace(idx):
            yield


def _attn_prompt_tile(sinks_ref, q_ref, kvp_ref, kvo_ref, o_ref, kv_scr, vt_scr, first_of_seq):
    rows = q_ref.shape[0]
    kv_scr[0:WINDOW, :] = kvp_ref[...].astype(BF16)
    kv_scr[WINDOW:WINDOW + rows, :] = kvo_ref[...].astype(BF16)
    vt_scr[:, 0:WINDOW] = kvp_ref[:, A_KV:2 * A_KV].T.astype(BF16)
    vt_scr[:, WINDOW:WINDOW + rows] = kvo_ref[:, A_KV:2 * A_KV].T.astype(BF16)
    band = WINDOW + CHUNK
    n_missing = jnp.where(first_of_seq, WINDOW, 0)
    problems = []
    for i in range(rows // CHUNK):
        first_valid = (n_missing - i * CHUNK) if i * CHUNK < WINDOW else None
        for g in range(A_KV_HEADS):
            problems.append((i * CHUNK, CHUNK, i * CHUNK, band, g, first_valid))
    yield
    yield from _attn_groups(sinks_ref, q_ref, o_ref, kv_scr, vt_scr, problems)


def _attn_sample_body(sinks_ref, q_ref, ck_ref, cv_ref, kv_ref, o_ref, kv_scr):
    rows = ck_ref.shape[1]
    t = q_ref.shape[1]
    kv_scr[0:rows, 0:A_KV] = ck_ref[0].astype(BF16)
    kv_scr[0:rows, A_KV:2 * A_KV] = cv_ref[0].astype(BF16)
    kv_scr[rows:rows + t, :] = kv_ref[0].astype(BF16)
    _run(_attn_groups(sinks_ref, q_ref.at[0], o_ref.at[0], kv_scr, None,
                      [(0, t, 0, rows + t, g, None) for g in range(A_KV_HEADS)]))


def _attn_sample(q, cache_k, cache_v, kv, sinks, *, layer):
    nb, t, _ = q.shape
    rows = cache_k.shape[2]
    return pl.pallas_call(
        _attn_sample_body,
        grid=(nb,),
        in_specs=[
            pl.BlockSpec(memory_space=pltpu.SMEM),
            pl.BlockSpec((1, t, D_MODEL), lambda b: (b, 0, 0)),
            pl.BlockSpec((None, 1, rows, A_KV), lambda b: (layer, b, 0, 0)),
            pl.BlockSpec((None, 1, rows, A_KV), lambda b: (layer, b, 0, 0)),
            pl.BlockSpec((1, t, 2 * A_KV), lambda b: (b, 0, 0)),
        ],
        out_specs=pl.BlockSpec((1, t, D_MODEL), lambda b: (b, 0, 0)),
        out_shape=jax.ShapeDtypeStruct((nb, t, D_MODEL), BF16),
        scratch_shapes=[pltpu.VMEM((rows + t, 2 * A_KV), BF16)],
        compiler_params=_params(1, 32),
        name="swa_sample_attn",
    )(sinks, q, cache_k, cache_v, kv)


def _rope_tables(pos):
    inv = ROPE_THETA ** (-jnp.arange(0, A_HD, 2, dtype=F32) / A_HD)
    ang = pos.astype(F32)[:, None] * inv[None, :]
    cos = jnp.cos(ang)
    sin = jnp.sin(ang)
    reps = LANES // A_HD
    return (jnp.concatenate([cos, cos] * reps, axis=-1),
            jnp.concatenate([-sin, sin] * reps, axis=-1))


def kernel(x_prompt, x_sample, state_mlstm_C, state_mlstm_n, state_mlstm_m, cache_swa_k, cache_swa_v,
           ffn_norm1, ffn_w_in1, ffn_w_out1, mix_norm, mlstm_w_in, mlstm_b_gates, mlstm_head_norm,
           mlstm_w_out, swa_w_qkv, swa_sinks, swa_w_out, ffn_norm2, ffn_w_in2, ffn_w_out2, final_norm):
    bp, tp, _ = x_prompt.shape
    bs, ts, _ = x_sample.shape
    tm_p = 512
    tm_s = bs * ts

    w_in1, w_out1 = ffn_w_in1.astype(BF16), ffn_w_out1.astype(BF16)
    w_in2, w_out2 = ffn_w_in2.astype(BF16), ffn_w_out2.astype(BF16)
    m_w_main = mlstm_w_in[:, :, :M_MAIN].astype(BF16)
    n_gates = 2 * M_HEADS
    m_w_gates = jnp.pad(mlstm_w_in[:, :, M_MAIN:], ((0, 0), (0, 0), (0, LANES - n_gates))).astype(BF16)
    m_b_gates = jnp.pad(mlstm_b_gates.astype(F32), ((0, 0), (0, LANES - n_gates)))[:, None, :]
    m_w_out = mlstm_w_out.astype(BF16)
    s_wq = swa_w_qkv[:, :, :D_MODEL].astype(BF16)
    s_wkv = swa_w_qkv[:, :, D_MODEL:].astype(BF16)
    s_w_out = swa_w_out.astype(BF16)
    sinks = swa_sinks.astype(F32)
    rows3 = lambda a: a.astype(F32)[:, None, :]
    g1, g2, gmix, hnorm = rows3(ffn_norm1), rows3(ffn_norm2), rows3(mix_norm), rows3(mlstm_head_norm)
    gf = final_norm.astype(F32)[None, :]

    cos_p, sin_p = _rope_tables(jnp.arange(tp))
    cos_s, sin_s = _rope_tables(PAST_LEN + jnp.arange(ts))
    cos_s, sin_s = jnp.tile(cos_s, (bs, 1)), jnp.tile(sin_s, (bs, 1))

    yp = x_prompt.reshape(bp * tp, D_MODEL)
    ys = x_sample.reshape(bs * ts, D_MODEL)
    c0_s = state_mlstm_C.astype(F32)
    n0_s = state_mlstm_n.astype(F32)
    m0_s = jnp.broadcast_to(state_mlstm_m.astype(F32)[..., None], state_mlstm_m.shape + (LANES,))
    rows = cache_swa_k.shape[2]
    ck = cache_swa_k.astype(F32).reshape(-1, bs, rows, A_KV)
    cv = cache_swa_v.astype(F32).reshape(-1, bs, rows, A_KV)

    p_c, p_n, p_m, p_k, p_v = [], [], [], [], []
    s_c, s_n, s_m, s_k, s_v = [], [], [], [], []
    for i in range(DEPTH):
        j = i // 2
        if i % 2 == 0:
            proj_p = proj_s = ("mlstm", gmix, i, m_w_main, m_w_gates, m_b_gates, j)
        else:
            proj_p = ("swa", gmix, i, s_wq, s_wkv, cos_p, sin_p, j)
            proj_s = ("swa", gmix, i, s_wq, s_wkv, cos_s, sin_s, j)
        yp, pa_p, pb_p = _ffn(yp, g1, w_in1, w_out1, gf, layer=i, tm=tm_p, proj=proj_p)
        ys, pa_s, pb_s = _ffn(ys, g1, w_in1, w_out1, gf, layer=i, tm=tm_s, proj=proj_s)
        if i % 2 == 0:
            mix_p = dict(scan=(pa_p, pb_p, hnorm, m_w_out, j, tp // tm_p))

            qkvo, gates = pa_s, pb_s
            hg, c, n, m = _mlstm(qkvo.reshape(bs, ts, M_MAIN), gates.reshape(bs, ts, LANES),
                                 c0_s, n0_s, m0_s, hnorm, layer=j, bg=bs,
                                 L=min(CHUNK, ts))
            mix_s = (hg.reshape(bs * ts, M_V), m_w_out, j)
            s_c.append(c); s_n.append(n); s_m.append(m[:, :, 0])
        else:
            q, kv = pa_p, pb_p
            kv3 = kv.reshape(bp, tp, 2 * A_KV)
            mix_p = dict(attn=(q, kv, sinks[j], s_w_out, j, tp // tm_p))
            keep = min(WINDOW, tp)
            p_k.append(kv3[:, tp - keep:, :A_KV].reshape(bp, keep, A_KV_HEADS, A_HD))
            p_v.append(kv3[:, tp - keep:, A_KV:].reshape(bp, keep, A_KV_HEADS, A_HD))

            q, kv = pa_s, pb_s
            kv3 = kv.reshape(bs, ts, 2 * A_KV)
            o = _attn_sample(q.reshape(bs, ts, D_MODEL), ck, cv, kv3, sinks[j], layer=j)
            mix_s = (o.reshape(bs * ts, D_MODEL), s_w_out, j)
            s_k.append(kv3[:, :, :A_KV].reshape(bs, ts, A_KV_HEADS, A_HD))
            s_v.append(kv3[:, :, A_KV:].reshape(bs, ts, A_KV_HEADS, A_HD))
        last = i == DEPTH - 1
        yp, *state = _ffn(yp, g2, w_in2, w_out2, gf, layer=i, tm=tm_p, final=last, **mix_p)
        if state:
            c, n, m = state
            p_c.append(c); p_n.append(n); p_m.append(m[:, :, 0])
        ys, = _ffn(ys, g2, w_in2, w_out2, gf, layer=i, tm=tm_s, final=last, mixer_out=mix_s)

    return (yp.reshape(bp, tp, D_MODEL), ys.reshape(bs, ts, D_MODEL),
            jnp.stack(p_c), jnp.stack(p_n), jnp.stack(p_m), jnp.stack(p_k), jnp.stack(p_v),
            jnp.stack(s_c), jnp.stack(s_n), jnp.stack(s_m), jnp.stack(s_k), jnp.stack(s_v))
```

```python
import functools
import itertools

import jax
import jax.numpy as jnp
from jax import lax
from jax.experimental import pallas as pl
from jax.experimental.pallas import tpu as pltpu

F32 = jnp.float32
BF16 = jnp.bfloat16

D_MODEL = 1024
DEPTH = 4
CHUNK = 64
M_HEADS = 4
M_DK = D_MODEL // 8
M_DV = D_MODEL // M_HEADS
M_QK = M_HEADS * M_DK
M_V = M_HEADS * M_DV
M_MAIN = 2 * M_QK + 2 * M_V
A_HEADS = 16
A_KV_HEADS = 4
A_HD = D_MODEL // A_HEADS
A_GROUP = A_HEADS // A_KV_HEADS
A_KV = A_KV_HEADS * A_HD
WINDOW = 128
PAST_LEN = 4096
ROPE_THETA = 10000.0
D_FF = 11 * D_MODEL // 4
FFN_RES = 0.5
EPS = 1e-6

LANES = 128
FFN_CHUNK = 256
PROJ_CHUNK = 512
M_SCAN_CHUNK = 256
ATTN_PACE = 2
SCAN_PACE = 1
MIB = 1024 * 1024


def _rms(x, g):
    return x * lax.rsqrt(jnp.mean(x * x, axis=-1, keepdims=True) + EPS) * g


def _const_spec(shape):
    nd = len(shape)
    return pl.BlockSpec(shape, lambda *_: (0,) * nd, pipeline_mode=pl.Buffered(1))


def _layer_spec(shape, layer):
    nd = len(shape)
    return pl.BlockSpec((None, *shape), lambda *_: (layer,) + (0,) * nd, pipeline_mode=pl.Buffered(1))


def _run(work):
    for _ in work:
        pass


def _params(n_grid, vmem_mib):
    return pltpu.CompilerParams(
        dimension_semantics=("arbitrary",) * n_grid,
        vmem_limit_bytes=vmem_mib * MIB)


def _mlstm_in_proj(xn, w_ref, wg_ref, bg_ref, o_ref, og_ref):
    for c in range(M_MAIN // PROJ_CHUNK):
        lo = c * PROJ_CHUNK
        o_ref[:, lo:lo + PROJ_CHUNK] = jnp.dot(
            xn, w_ref[:, lo:lo + PROJ_CHUNK], preferred_element_type=F32).astype(BF16)
    og_ref[...] = jnp.dot(xn, wg_ref[...], preferred_element_type=F32) + bg_ref[...]


def _rope(x, cos, sin_signed, first_half):
    swapped = jnp.where(first_half, pltpu.roll(x, LANES - A_HD // 2, 1), pltpu.roll(x, A_HD // 2, 1))
    return x * cos + swapped * sin_signed


def _swa_in_proj(xn, wq_ref, wkv_ref, cos_ref, sin_ref, q_ref, kv_ref):
    cos = cos_ref[...]
    sin_signed = sin_ref[...]
    lane = lax.broadcasted_iota(jnp.int32, cos.shape, 1)
    first_half = (lane & (A_HD // 2)) == 0
    q_scale = A_HD ** -0.5
    for c in range(D_MODEL // PROJ_CHUNK):
        q = jnp.dot(xn, wq_ref[:, c * PROJ_CHUNK:(c + 1) * PROJ_CHUNK], preferred_element_type=F32)
        for j in range(PROJ_CHUNK // LANES):
            lo = c * PROJ_CHUNK + j * LANES
            blk = _rope(q[:, j * LANES:(j + 1) * LANES], cos, sin_signed, first_half)
            q_ref[:, lo:lo + LANES] = (blk * q_scale).astype(BF16)
    kv = jnp.dot(xn, wkv_ref[...], preferred_element_type=F32)
    for j in range(A_KV // LANES):
        kv_ref[:, j * LANES:(j + 1) * LANES] = _rope(
            kv[:, j * LANES:(j + 1) * LANES], cos, sin_signed, first_half)
    kv_ref[:, A_KV:2 * A_KV] = kv[:, A_KV:2 * A_KV]


def _ffn_body(*refs, mixer_out, final, proj, attn, scan):
    refs = iter(refs)
    x_ref = next(refs)
    if attn:
        q_ref, kvp_ref, kvo_ref, sinks_ref, wmix_ref = [next(refs) for _ in range(5)]
    elif scan:
        qkvo_ref, gates_ref, hn_ref, wmix_ref = [next(refs) for _ in range(4)]
    if attn or scan:
        x0_ref, a0_ref = next(refs), next(refs)
    elif mixer_out:
        a_ref, wmix_ref = next(refs), next(refs)
    g_ref, win_ref, wout_ref, gf_ref = next(refs), next(refs), next(refs), next(refs)
    if proj:
        gmix_ref = next(refs)
        proj_refs = [next(refs) for _ in range(3 if proj == "mlstm" else 4)]
    o_ref = next(refs)
    o0_ref = next(refs) if attn or scan else None
    if proj:
        proj_refs += [next(refs), next(refs)]
    if scan:
        c_ref, n_ref, m_ref = next(refs), next(refs), next(refs)
    h_ref = next(refs)

    if attn or scan:
        n_tiles, tiles_per_seq = attn or scan
        a_scr = next(refs)
        s = pl.program_id(0)

        @pl.when(s == 0)
        def _():
            a_scr[1] = a0_ref[...]

        slot = lax.rem(s, 2)
        a = a_scr[1 - slot]
        x = jnp.where(s == 0, x0_ref[...], x_ref[...]) + jnp.dot(a, wmix_ref[...],
                                                                    preferred_element_type=F32)
        first_of_seq = lax.rem(jnp.minimum(s, n_tiles - 1), tiles_per_seq) == 0
        if attn:
            kv_scr, vt_scr = next(refs), next(refs)
            mixer_work = _attn_prompt_tile(sinks_ref, q_ref, kvp_ref, kvo_ref, a_scr.at[slot], kv_scr,
                                           vt_scr, first_of_seq)
            mixer_pace = ATTN_PACE
        else:
            live = s < n_tiles

            @pl.when(first_of_seq & live)
            def _():
                c_ref[...] = jnp.zeros_like(c_ref)
                n_ref[...] = jnp.zeros_like(n_ref)
                m_ref[...] = jnp.zeros_like(m_ref)

            L = M_SCAN_CHUNK
            mixer_work = itertools.chain.from_iterable(
                _mlstm_chunk([dict(qkvo=qkvo_ref.at[pl.ds(lo, L)], gates=gates_ref.at[pl.ds(lo, L)],
                                   hg=a_scr.at[slot, pl.ds(lo, L)],
                                   c=c_ref.at[0], n=n_ref.at[0], m=m_ref.at[0])],
                             hn_ref, L, keep=live)
                for lo in range(0, x_ref.shape[0], L))
            mixer_pace = SCAN_PACE
    else:
        mixer_work, mixer_pace = iter(()), 0
        if mixer_out:
            x = x_ref[...] + jnp.dot(a_ref[...], wmix_ref[...], preferred_element_type=F32)
        else:
            x = x_ref[...]

    xn = _rms(x, g_ref[...]).astype(BF16)
    for c in range(D_FF // FFN_CHUNK):
        lo = c * FFN_CHUNK
        gate = jnp.dot(xn, win_ref[:, lo:lo + FFN_CHUNK], preferred_element_type=F32)
        up = jnp.dot(xn, win_ref[:, D_FF + lo:D_FF + lo + FFN_CHUNK], preferred_element_type=F32)
        h_ref[:, lo:lo + FFN_CHUNK] = (gate * jax.nn.sigmoid(gate) * up).astype(BF16)
        for _ in range(mixer_pace):
            next(mixer_work, None)
    _run(mixer_work)
    y = x + FFN_RES * jnp.dot(h_ref[...], wout_ref[...], preferred_element_type=F32)
    if final:
        y = _rms(y, gf_ref[...])
    o_ref[...] = y
    if o0_ref is not None:
        o0_ref[...] = y
    if proj and not final:
        yn = _rms(y, gmix_ref[...]).astype(BF16)
        (_mlstm_in_proj if proj == "mlstm" else _swa_in_proj)(yn, *proj_refs)


def _ffn(x, g, w_in, w_out, g_final, *, layer, tm, final=False, mixer_out=None, attn=None, scan=None,
         side=None, proj=None):
    n = x.shape[0]
    n_tiles = n // tm
    n_steps = n_tiles
    tile = lambda width: pl.BlockSpec((tm, width), lambda i: (i, 0))
    scratch = [pltpu.VMEM((tm, D_FF), BF16)]
    body_attn = body_scan = None
    if attn or scan:
        n_steps = n_tiles + 1
        tile = lambda width: pl.BlockSpec((tm, width), lambda i: (jnp.maximum(i - 1, 0), 0))
        ahead = lambda width: pl.BlockSpec((tm, width), lambda i: (jnp.minimum(i, n_tiles - 1), 0))
        scratch.append(pltpu.VMEM((2, tm, D_MODEL), BF16))
    args, specs = [x], [tile(D_MODEL)]
    if attn:
        q, kv, sinks, w_mix, mix_layer, tiles_per_seq = attn
        body_attn = (n_tiles, tiles_per_seq)
        prev_blocks = tm // WINDOW
        prev = pl.BlockSpec(
            (WINDOW, 2 * A_KV), lambda i: (jnp.maximum(jnp.minimum(i, n_tiles - 1) * prev_blocks - 1, 0), 0))
        scratch += [pltpu.VMEM((WINDOW + tm, 2 * A_KV), BF16), pltpu.VMEM((A_KV, WINDOW + tm), BF16)]
        args += [q, kv, kv, sinks, w_mix]
        specs += [ahead(D_MODEL), prev, ahead(2 * A_KV), pl.BlockSpec(memory_space=pltpu.SMEM),
                  _layer_spec((D_MODEL, D_MODEL), mix_layer)]
    elif scan:
        qkvo, gates, hnorm, w_mix, mix_layer, tiles_per_seq = scan
        body_scan = (n_tiles, tiles_per_seq)
        args += [qkvo, gates, hnorm, w_mix]
        specs += [ahead(M_MAIN), ahead(LANES), _layer_spec((1, M_V), mix_layer),
                  _layer_spec((D_MODEL, D_MODEL), mix_layer)]
    if attn or scan:
        x0, a0 = side
        args += [x0, a0]
        specs += [pl.BlockSpec((tm, D_MODEL), lambda i: (0, 0))] * 2
    elif mixer_out:
        a, w_mix, mix_layer = mixer_out
        args += [a, w_mix]
        specs += [tile(D_MODEL), _layer_spec((D_MODEL, D_MODEL), mix_layer)]
    args += [g, w_in, w_out, g_final]
    specs += [_layer_spec((1, D_MODEL), layer), _layer_spec((D_MODEL, 2 * D_FF), layer),
              _layer_spec((D_FF, D_MODEL), layer), _const_spec((1, D_MODEL))]
    out_specs = [tile(D_MODEL)]
    out_shape = [jax.ShapeDtypeStruct((n, D_MODEL), F32)]
    if attn or scan:
        out_specs.append(pl.BlockSpec((tm, D_MODEL), lambda i: (jnp.minimum(i, 1), 0)))
        out_shape.append(jax.ShapeDtypeStruct((2 * tm, D_MODEL), F32))
    kind = None
    if proj:
        kind, gmix, norm_layer = proj[:3]
        args.append(gmix)
        specs.append(_layer_spec((1, D_MODEL), norm_layer))
        if kind == "mlstm":
            w, wg, bg, mix_layer = proj[3:]
            args += [w, wg, bg]
            specs += [_layer_spec((D_MODEL, M_MAIN), mix_layer), _layer_spec((D_MODEL, LANES), mix_layer),
                      _layer_spec((1, LANES), mix_layer)]
            widths = [(M_MAIN, BF16), (LANES, F32)]
        else:
            wq, wkv, cos, sin_signed, mix_layer = proj[3:]
            n_tab = cos.shape[0] // tm
            table = pl.BlockSpec((tm, LANES), lambda i: (i % n_tab, 0))
            args += [wq, wkv, cos, sin_signed]
            specs += [_layer_spec((D_MODEL, D_MODEL), mix_layer), _layer_spec((D_MODEL, 2 * A_KV), mix_layer),
                      table, table]
            widths = [(D_MODEL, BF16), (2 * A_KV, F32)]
        out_specs += [tile(wd) for wd, _ in widths]
        out_shape += [jax.ShapeDtypeStruct((n, wd), dt) for wd, dt in widths]
    if scan:
        n_seqs = n_tiles // tiles_per_seq
        for shp in [(M_HEADS, M_DK, M_DV), (M_HEADS, M_DK), (M_HEADS, LANES)]:
            out_specs.append(pl.BlockSpec(
                (1, *shp),
                lambda i, nd=len(shp): (jnp.minimum(i, n_tiles - 1) // tiles_per_seq,) + (0,) * nd))
            out_shape.append(jax.ShapeDtypeStruct((n_seqs, *shp), F32))
    return pl.pallas_call(
        functools.partial(_ffn_body, mixer_out=bool(mixer_out), final=final, proj=kind, attn=body_attn,
                          scan=body_scan),
        grid=(n_steps,),
        in_specs=specs,
        out_specs=out_specs,
        out_shape=out_shape,
        scratch_shapes=scratch,
        compiler_params=_params(1, 56),
        name="half_ffn",
    )(*args)


def _mlstm_body(qkvo_ref, gates_ref, c0_ref, n0_ref, m0_ref, hn_ref,
                hg_ref, c_ref, n_ref, m_ref, *, bg, L):
    @pl.when(pl.program_id(1) == 0)
    def _():
        c_ref[...] = c0_ref[...]
        n_ref[...] = n0_ref[...]
        m_ref[...] = m0_ref[...]

    _run(_mlstm_chunk([dict(qkvo=qkvo_ref.at[b], gates=gates_ref.at[b], hg=hg_ref.at[b],
                            c=c_ref.at[b], n=n_ref.at[b], m=m_ref.at[b]) for b in range(bg)],
                      hn_ref, L))


def _mlstm_chunk(seqs, hn_ref, L, keep=None):
    scale = M_DK ** -0.5
    row = lax.broadcasted_iota(jnp.int32, (L, L), 0)
    col = lax.broadcasted_iota(jnp.int32, (L, L), 1)
    causal = col <= row
    tril = causal.astype(BF16)
    lane = lax.broadcasted_iota(jnp.int32, (L, LANES), 1)

    probs = []
    for seq in seqs:
        gts = seq["gates"][...]
        lf = jax.nn.log_sigmoid(gts)
        lf_hi = lf.astype(BF16)
        r1 = lf - lf_hi.astype(F32)
        lf_mid = r1.astype(BF16)
        lf_lo = (r1 - lf_mid.astype(F32)).astype(BF16)
        bcs = (jnp.dot(tril, lf_hi, preferred_element_type=F32)
               + jnp.dot(tril, lf_mid, preferred_element_type=F32)
               + jnp.dot(tril, lf_lo, preferred_element_type=F32))
        both = jnp.where(lane < M_HEADS, gts, bcs)
        if L % LANES:
            both = jnp.concatenate([both, jnp.zeros((LANES - L, LANES), F32)], axis=0)
        both_t = both.T
        for h in range(M_HEADS):
            p = dict(seq=seq, h=h)
            b_row = both_t[M_HEADS + h:M_HEADS + h + 1, 0:L]
            p["b_last"] = b_row[:, L - 1:L]
            p["bmi_row"] = b_row - both_t[h:h + 1, 0:L]
            b_col = bcs[:, M_HEADS + h:M_HEADS + h + 1]
            p["b_rep"] = jnp.broadcast_to(b_col, (L, LANES))
            p["c_rep"] = jnp.broadcast_to(gts[:, h:h + 1] - b_col, (L, LANES))
            qkvo_ref = seq["qkvo"]
            p["m_prev"] = seq["m"][h:h + 1, 0:1]
            p["n_row"] = seq["n"][h:h + 1, :]
            p["c_prev"] = seq["c"][h]
            p["q"] = qkvo_ref[:, h * M_DK:(h + 1) * M_DK]
            p["k"] = qkvo_ref[:, M_QK + h * M_DK:M_QK + (h + 1) * M_DK]
            p["v"] = qkvo_ref[:, 2 * M_QK + h * M_DV:2 * M_QK + (h + 1) * M_DV]
            probs.append(p)

    def wide(c, width):
        return c[:, :width] if width <= LANES else jnp.concatenate([c] * (width // LANES), axis=1)

    def fold(x):
        acc = x[:, :LANES]
        for t in range(1, x.shape[1] // LANES):
            acc = acc + x[:, t * LANES:(t + 1) * LANES]
        return acc

    yield
    for p in probs:
        p["s"] = lax.dot_general(p["q"], p["k"], (((1,), (1,)), ((), ())), preferred_element_type=F32)
        p["qc"] = jnp.dot(p["q"], p["c_prev"].astype(BF16), preferred_element_type=F32)
        n_rep = jnp.broadcast_to(p["n_row"], (LANES, M_DK)).astype(BF16)
        p["qn"] = lax.dot_general(p["q"], n_rep, (((1,), (1,)), ((), ())), preferred_element_type=F32)

    yield
    for p in probs:
        d = jnp.where(causal, wide(p["b_rep"], L) - p["bmi_row"], -jnp.inf)
        g_rep = p["b_rep"] + p["m_prev"]
        m_t = jnp.maximum(g_rep, jnp.max(d, axis=-1, keepdims=True))
        w = jnp.exp(d - wide(m_t, L)) * (p["s"] * scale)
        p["m_t"] = m_t
        p["inter"] = jnp.exp(g_rep - m_t)
        p["w_sum"] = jnp.sum(fold(w), axis=-1, keepdims=True)
        p["w"] = w.astype(BF16)
        b_last = p["b_last"]
        a_row = b_last - p["bmi_row"]
        m_new = jnp.maximum(p["m_prev"] + b_last, jnp.max(a_row, axis=-1, keepdims=True))
        p["m_new"] = m_new
        p["decay"] = jnp.exp(p["m_prev"] + b_last - m_new)
        p["wk"] = (jnp.exp(b_last + p["c_rep"] - m_new) * scale) * p["k"].astype(F32)

    yield
    for p in probs:
        p["wv"] = jnp.dot(p["w"], p["v"], preferred_element_type=F32)
        p["kv"] = lax.dot_general(p["wk"].astype(BF16), p["v"], (((0,), (0,)), ((), ())),
                                  preferred_element_type=F32)

    def updated(new, old):
        return new if keep is None else jnp.where(keep, new, old)

    yield
    for p in probs:
        seq, h = p["seq"], p["h"]
        num = wide(p["inter"], M_DV) * p["qc"] + p["wv"]
        den = p["inter"] * p["qn"] + p["w_sum"]
        hh = num / wide(jnp.maximum(jnp.abs(den), jnp.exp(-p["m_t"])), M_DV)
        hh = hh * lax.rsqrt(jnp.mean(hh * hh, axis=-1, keepdims=True) + EPS)
        hh = hh * hn_ref[:, h * M_DV:(h + 1) * M_DV]
        og = seq["qkvo"][:, 2 * M_QK + M_V + h * M_DV:2 * M_QK + M_V + (h + 1) * M_DV]
        seq["hg"][:, h * M_DV:(h + 1) * M_DV] = (jax.nn.sigmoid(og.astype(F32)) * hh).astype(BF16)
        seq["c"][h] = updated(p["decay"] * p["c_prev"] + p["kv"], p["c_prev"])
        seq["n"][h:h + 1, :] = updated(
            p["decay"] * p["n_row"] + jnp.sum(p["wk"], axis=0, keepdims=True), p["n_row"])
        m_row = jnp.broadcast_to(p["m_new"], (1, LANES))
        seq["m"][h:h + 1, :] = updated(m_row, jnp.broadcast_to(p["m_prev"], (1, LANES)))


def _mlstm(qkvo, gates, c0, n0, m0, hnorm, *, layer, bg, L):
    nb, t, _ = qkvo.shape
    state_shapes = [(bg, M_HEADS, M_DK, M_DV), (bg, M_HEADS, M_DK), (bg, M_HEADS, LANES)]
    state_specs = [pl.BlockSpec(shp, lambda g, c, nd=len(shp): (g,) + (0,) * (nd - 1))
                   for shp in state_shapes]
    init_specs = [pl.BlockSpec((None, *shp), lambda g, c, nd=len(shp): (layer, g) + (0,) * (nd - 1))
                  for shp in state_shapes]
    return pl.pallas_call(
        functools.partial(_mlstm_body, bg=bg, L=L),
        grid=(nb // bg, t // L),
        in_specs=[
            pl.BlockSpec((bg, L, M_MAIN), lambda g, c: (g, c, 0)),
            pl.BlockSpec((bg, L, LANES), lambda g, c: (g, c, 0)),
            *init_specs,
            _layer_spec((1, M_V), layer),
        ],
        out_specs=[pl.BlockSpec((bg, L, M_V), lambda g, c: (g, c, 0)), *state_specs],
        out_shape=[
            jax.ShapeDtypeStruct((nb, t, M_V), BF16),
            jax.ShapeDtypeStruct((nb, M_HEADS, M_DK, M_DV), F32),
            jax.ShapeDtypeStruct((nb, M_HEADS, M_DK), F32),
            jax.ShapeDtypeStruct((nb, M_HEADS, LANES), F32),
        ],
        compiler_params=_params(2, 32),
        name="mlstm_scan",
    )(qkvo, gates, c0, n0, m0, hnorm)


def _attn_groups(sinks_ref, q_ref, o_ref, kv_scr, vt_scr, problems):
    per_slice = 8

    def pace(idx):
        return (idx + 1) % per_slice == 0

    scores = []
    for idx, (q_lo, q_rows, k_lo, k_rows, g, first_valid) in enumerate(problems):
        kb = kv_scr[k_lo:k_lo + k_rows, g * A_HD:(g + 1) * A_HD]
        qs = jnp.concatenate(
            [q_ref[q_lo:q_lo + q_rows, hd * A_HD:(hd + 1) * A_HD]
             for hd in range(A_GROUP * g, A_GROUP * (g + 1))], axis=0)
        scores.append(lax.dot_general(kb, qs, (((1,), (1,)), ((), ())), preferred_element_type=F32))
        if pace(idx):
            yield
    probs = []
    for idx, (s, (q_lo, q_rows, k_lo, k_rows, g, first_valid)) in enumerate(zip(scores, problems)):
        if first_valid is not None:
            key = lax.broadcasted_iota(jnp.int32, s.shape, 0)
            s = jnp.where(key >= first_valid, s, -jnp.inf)
        sk = jnp.concatenate([jnp.full((1, q_rows), sinks_ref[hd], F32)
                              for hd in range(A_GROUP * g, A_GROUP * (g + 1))], axis=1)
        m = jnp.maximum(jnp.max(s, axis=0, keepdims=True), sk)
        p = jnp.exp(s - m)
        den = jnp.sum(p, axis=0, keepdims=True) + jnp.exp(sk - m)
        probs.append((p.astype(BF16), den))
        if pace(idx):
            yield
    outs = []
    for idx, ((p, den), (q_lo, q_rows, k_lo, k_rows, g, first_valid)) in enumerate(zip(probs, problems)):
        if vt_scr is None:
            vb = kv_scr[k_lo:k_lo + k_rows, A_KV + g * A_HD:A_KV + (g + 1) * A_HD]
            o_t = lax.dot_general(vb, p, (((0,), (0,)), ((), ())), preferred_element_type=F32)
        else:
            o_t = jnp.dot(vt_scr[g * A_HD:(g + 1) * A_HD, k_lo:k_lo + k_rows], p,
                          preferred_element_type=F32)
        outs.append(o_t / den)
        if pace(idx):
            yield
    for idx, (o_t, (q_lo, q_rows, k_lo, k_rows, g, first_valid)) in enumerate(zip(outs, problems)):
        o = o_t.T
        for j in range(A_GROUP):
            hd = A_GROUP * g + j
            o_ref[q_lo:q_lo + q_rows, hd * A_HD:(hd + 1) * A_HD] = (
                o[j * q_rows:(j + 1) * q_rows].astype(BF16))
        if pace(idx):
            yield


def _attn_prompt_tile(sinks_ref, q_ref, kvp_ref, kvo_ref, o_ref, kv_scr, vt_scr, first_of_seq):
    rows = q_ref.shape[0]
    kv_scr[0:WINDOW, :] = kvp_ref[...].astype(BF16)
    kv_scr[WINDOW:WINDOW + rows, :] = kvo_ref[...].astype(BF16)
    vt_scr[:, 0:WINDOW] = kvp_ref[:, A_KV:2 * A_KV].T.astype(BF16)
    vt_scr[:, WINDOW:WINDOW + rows] = kvo_ref[:, A_KV:2 * A_KV].T.astype(BF16)
    band = WINDOW + CHUNK
    n_missing = jnp.where(first_of_seq, WINDOW, 0)
    problems = []
    for i in range(rows // CHUNK):
        first_valid = (n_missing - i * CHUNK) if i * CHUNK < WINDOW else None
        for g in range(A_KV_HEADS):
            problems.append((i * CHUNK, CHUNK, i * CHUNK, band, g, first_valid))
    yield
    yield from _attn_groups(sinks_ref, q_ref, o_ref, kv_scr, vt_scr, problems)


def _attn_sample_body(sinks_ref, q_ref, ck_ref, cv_ref, kv_ref, o_ref, kv_scr):
    rows = ck_ref.shape[1]
    t = q_ref.shape[1]
    kv_scr[0:rows, 0:A_KV] = ck_ref[0].astype(BF16)
    kv_scr[0:rows, A_KV:2 * A_KV] = cv_ref[0].astype(BF16)
    kv_scr[rows:rows + t, :] = kv_ref[0].astype(BF16)
    _run(_attn_groups(sinks_ref, q_ref.at[0], o_ref.at[0], kv_scr, None,
                      [(0, t, 0, rows + t, g, None) for g in range(A_KV_HEADS)]))


def _attn_sample(q, cache_k, cache_v, kv, sinks, *, layer):
    nb, t, _ = q.shape
    rows = cache_k.shape[2]
    return pl.pallas_call(
        _attn_sample_body,
        grid=(nb,),
        in_specs=[
            pl.BlockSpec(memory_space=pltpu.SMEM),
            pl.BlockSpec((1, t, D_MODEL), lambda b: (b, 0, 0)),
            pl.BlockSpec((None, 1, rows, A_KV), lambda b: (layer, b, 0, 0)),
            pl.BlockSpec((None, 1, rows, A_KV), lambda b: (layer, b, 0, 0)),
            pl.BlockSpec((1, t, 2 * A_KV), lambda b: (b, 0, 0)),
        ],
        out_specs=pl.BlockSpec((1, t, D_MODEL), lambda b: (b, 0, 0)),
        out_shape=jax.ShapeDtypeStruct((nb, t, D_MODEL), BF16),
        scratch_shapes=[pltpu.VMEM((rows + t, 2 * A_KV), BF16)],
        compiler_params=_params(1, 32),
        name="swa_sample_attn",
    )(sinks, q, cache_k, cache_v, kv)


def _rope_tables(pos):
    inv = ROPE_THETA ** (-jnp.arange(0, A_HD, 2, dtype=F32) / A_HD)
    ang = pos.astype(F32)[:, None] * inv[None, :]
    cos = jnp.cos(ang)
    sin = jnp.sin(ang)
    reps = LANES // A_HD
    return (jnp.concatenate([cos, cos] * reps, axis=-1),
            jnp.concatenate([-sin, sin] * reps, axis=-1))


def kernel(x_prompt, x_sample, state_mlstm_C, state_mlstm_n, state_mlstm_m, cache_swa_k, cache_swa_v,
           ffn_norm1, ffn_w_in1, ffn_w_out1, mix_norm, mlstm_w_in, mlstm_b_gates, mlstm_head_norm,
           mlstm_w_out, swa_w_qkv, swa_sinks, swa_w_out, ffn_norm2, ffn_w_in2, ffn_w_out2, final_norm):
    bp, tp, _ = x_prompt.shape
    bs, ts, _ = x_sample.shape
    tm_p = 512
    tm_s = bs * ts

    w_in1, w_out1 = ffn_w_in1.astype(BF16), ffn_w_out1.astype(BF16)
    w_in2, w_out2 = ffn_w_in2.astype(BF16), ffn_w_out2.astype(BF16)
    m_w_main = mlstm_w_in[:, :, :M_MAIN].astype(BF16)
    n_gates = 2 * M_HEADS
    m_w_gates = jnp.pad(mlstm_w_in[:, :, M_MAIN:], ((0, 0), (0, 0), (0, LANES - n_gates))).astype(BF16)
    m_b_gates = jnp.pad(mlstm_b_gates.astype(F32), ((0, 0), (0, LANES - n_gates)))[:, None, :]
    m_w_out = mlstm_w_out.astype(BF16)
    s_wq = swa_w_qkv[:, :, :D_MODEL].astype(BF16)
    s_wkv = swa_w_qkv[:, :, D_MODEL:].astype(BF16)
    s_w_out = swa_w_out.astype(BF16)
    sinks = swa_sinks.astype(F32)
    rows3 = lambda a: a.astype(F32)[:, None, :]
    g1, g2, gmix, hnorm = rows3(ffn_norm1), rows3(ffn_norm2), rows3(mix_norm), rows3(mlstm_head_norm)
    gf = final_norm.astype(F32)[None, :]

    cos_p, sin_p = _rope_tables(jnp.arange(tp))
    cos_s, sin_s = _rope_tables(PAST_LEN + jnp.arange(ts))
    cos_s, sin_s = jnp.tile(cos_s, (bs, 1)), jnp.tile(sin_s, (bs, 1))

    yp = x_prompt.reshape(bp * tp, D_MODEL)
    ys = x_sample.reshape(bs * ts, D_MODEL)
    c0_s = state_mlstm_C.astype(F32)
    n0_s = state_mlstm_n.astype(F32)
    m0_s = jnp.broadcast_to(state_mlstm_m.astype(F32)[..., None], state_mlstm_m.shape + (LANES,))
    rows = cache_swa_k.shape[2]
    ck = cache_swa_k.astype(F32).reshape(-1, bs, rows, A_KV)
    cv = cache_swa_v.astype(F32).reshape(-1, bs, rows, A_KV)

    p_c, p_n, p_m, p_k, p_v = [], [], [], [], []
    s_c, s_n, s_m, s_k, s_v = [], [], [], [], []
    for i in range(DEPTH):
        j = i // 2
        if i % 2 == 0:
            proj_p = proj_s = ("mlstm", gmix, i, m_w_main, m_w_gates, m_b_gates, j)
        else:
            proj_p = ("swa", gmix, i, s_wq, s_wkv, cos_p, sin_p, j)
            proj_s = ("swa", gmix, i, s_wq, s_wkv, cos_s, sin_s, j)
        yp, pa_p, pb_p = _ffn(yp, g1, w_in1, w_out1, gf, layer=i, tm=tm_p, proj=proj_p)
        ys, pa_s, pb_s = _ffn(ys, g1, w_in1, w_out1, gf, layer=i, tm=tm_s, proj=proj_s)
        if i % 2 == 0:
            mix_p = dict(scan=(pa_p, pb_p, hnorm, m_w_out, j, tp // tm_p))

            qkvo, gates = pa_s, pb_s
            hg, c, n, m = _mlstm(qkvo.reshape(bs, ts, M_MAIN), gates.reshape(bs, ts, LANES),
                                 c0_s, n0_s, m0_s, hnorm, layer=j, bg=bs,
                                 L=min(CHUNK, ts))
            a_s = hg.reshape(bs * ts, M_V)
            s_c.append(c); s_n.append(n); s_m.append(m[:, :, 0])
        else:
            q, kv = pa_p, pb_p
            kv3 = kv.reshape(bp, tp, 2 * A_KV)
            mix_p = dict(attn=(q, kv, sinks[j], s_w_out, j, tp // tm_p))
            keep = min(WINDOW, tp)
            p_k.append(kv3[:, tp - keep:, :A_KV].reshape(bp, keep, A_KV_HEADS, A_HD))
            p_v.append(kv3[:, tp - keep:, A_KV:].reshape(bp, keep, A_KV_HEADS, A_HD))

            q, kv = pa_s, pb_s
            kv3 = kv.reshape(bs, ts, 2 * A_KV)
            o = _attn_sample(q.reshape(bs, ts, D_MODEL), ck, cv, kv3, sinks[j], layer=j)
            a_s = o.reshape(bs * ts, D_MODEL)
            s_k.append(kv3[:, :, :A_KV].reshape(bs, ts, A_KV_HEADS, A_HD))
            s_v.append(kv3[:, :, A_KV:].reshape(bs, ts, A_KV_HEADS, A_HD))
        last = i == DEPTH - 1
        grow = ((0, tm_p - tm_s), (0, 0))
        yp, side_out, *state = _ffn(yp, g2, w_in2, w_out2, gf, layer=i, tm=tm_p, final=last,
                                    side=(jnp.pad(ys, grow), jnp.pad(a_s, grow)), **mix_p)
        ys = side_out[:tm_s]
        if state:
            c, n, m = state
            p_c.append(c); p_n.append(n); p_m.append(m[:, :, 0])

    return (yp.reshape(bp, tp, D_MODEL), ys.reshape(bs, ts, D_MODEL),
            jnp.stack(p_c), jnp.stack(p_n), jnp.stack(p_m), jnp.stack(p_k), jnp.stack(p_v),
            jnp.stack(s_c), jnp.stack(s_n), jnp.stack(s_m), jnp.stack(s_k), jnp.stack(s_v))
```

```python
import functools
import itertools

import jax
import jax.numpy as jnp
from jax import lax
from jax.experimental import pallas as pl
from jax.experimental.pallas import tpu as pltpu

F32 = jnp.float32
BF16 = jnp.bfloat16

D_MODEL = 1024
DEPTH = 4
CHUNK = 64
M_HEADS = 4
M_DK = D_MODEL // 8
M_DV = D_MODEL // M_HEADS
M_QK = M_HEADS * M_DK
M_V = M_HEADS * M_DV
M_MAIN = 2 * M_QK + 2 * M_V
A_HEADS = 16
A_KV_HEADS = 4
A_HD = D_MODEL // A_HEADS
A_GROUP = A_HEADS // A_KV_HEADS
A_KV = A_KV_HEADS * A_HD
WINDOW = 128
PAST_LEN = 4096
ROPE_THETA = 10000.0
D_FF = 11 * D_MODEL // 4
FFN_RES = 0.5
EPS = 1e-6

LANES = 128
FFN_CHUNK = 256
PROJ_CHUNK = 512
FFN_SMALL_SPLIT = 2
M_SCAN_CHUNK = 256
MIB = 1024 * 1024


def _rms(x, g):
    return x * lax.rsqrt(jnp.mean(x * x, axis=-1, keepdims=True) + EPS) * g


def _const_spec(shape):
    nd = len(shape)
    return pl.BlockSpec(shape, lambda *_: (0,) * nd, pipeline_mode=pl.Buffered(1))


def _layer_spec(shape, layer):
    nd = len(shape)
    return pl.BlockSpec((None, *shape), lambda *_: (layer,) + (0,) * nd, pipeline_mode=pl.Buffered(1))


def _run(work):
    for _ in work:
        pass


def _params(n_grid, vmem_mib):
    return pltpu.CompilerParams(
        dimension_semantics=("arbitrary",) * n_grid,
        vmem_limit_bytes=vmem_mib * MIB)


def _mlstm_in_proj(xn, w_ref, wg_ref, bg_ref, o_ref, og_ref):
    for c in range(M_MAIN // PROJ_CHUNK):
        lo = c * PROJ_CHUNK
        o_ref[:, lo:lo + PROJ_CHUNK] = jnp.dot(
            xn, w_ref[:, lo:lo + PROJ_CHUNK], preferred_element_type=F32).astype(BF16)
    og_ref[...] = jnp.dot(xn, wg_ref[...], preferred_element_type=F32) + bg_ref[...]


def _rope(x, cos, sin_signed, first_half):
    swapped = jnp.where(first_half, pltpu.roll(x, LANES - A_HD // 2, 1), pltpu.roll(x, A_HD // 2, 1))
    return x * cos + swapped * sin_signed


def _swa_in_proj(xn, wq_ref, wkv_ref, cos_ref, sin_ref, q_ref, kv_ref):
    cos = cos_ref[...]
    sin_signed = sin_ref[...]
    lane = lax.broadcasted_iota(jnp.int32, cos.shape, 1)
    first_half = (lane & (A_HD // 2)) == 0
    q_scale = A_HD ** -0.5
    for c in range(D_MODEL // PROJ_CHUNK):
        q = jnp.dot(xn, wq_ref[:, c * PROJ_CHUNK:(c + 1) * PROJ_CHUNK], preferred_element_type=F32)
        for j in range(PROJ_CHUNK // LANES):
            lo = c * PROJ_CHUNK + j * LANES
            blk = _rope(q[:, j * LANES:(j + 1) * LANES], cos, sin_signed, first_half)
            q_ref[:, lo:lo + LANES] = (blk * q_scale).astype(BF16)
    kv = jnp.dot(xn, wkv_ref[...], preferred_element_type=F32)
    for j in range(A_KV // LANES):
        kv_ref[:, j * LANES:(j + 1) * LANES] = _rope(
            kv[:, j * LANES:(j + 1) * LANES], cos, sin_signed, first_half)
    kv_ref[:, A_KV:2 * A_KV] = kv[:, A_KV:2 * A_KV]


def _ffn_body(*refs, final, proj, attn, scan):
    refs = iter(refs)
    x_ref = next(refs)
    if attn:
        q_ref, kvp_ref, kvo_ref, sinks_ref, wmix_ref = [next(refs) for _ in range(5)]
    elif scan:
        qkvo_ref, gates_ref, hn_ref, wmix_ref = [next(refs) for _ in range(4)]
    g_ref, win_ref, wout_ref, gf_ref = next(refs), next(refs), next(refs), next(refs)
    if proj:
        gmix_ref = next(refs)
        proj_refs = [next(refs) for _ in range(3 if proj == "mlstm" else 4)]
    o_ref = next(refs)
    if proj:
        proj_refs += [next(refs), next(refs)]
    if scan:
        c_ref, n_ref, m_ref = next(refs), next(refs), next(refs)
    h_ref = next(refs)

    if attn or scan:
        n_tiles, tiles_per_seq = attn or scan
        a_scr = next(refs)
        s = pl.program_id(0)

        @pl.when(s == 0)
        def _():
            a_scr[...] = jnp.zeros_like(a_scr)

        slot = lax.rem(s, 2)
        a = a_scr[1 - slot]
        x = x_ref[...] + jnp.dot(a, wmix_ref[...], preferred_element_type=F32)
        first_of_seq = lax.rem(jnp.minimum(s, n_tiles - 1), tiles_per_seq) == 0
        if attn:
            kv_scr, vt_scr = next(refs), next(refs)
            mixer_work = _attn_prompt_tile(sinks_ref, q_ref, kvp_ref, kvo_ref, a_scr.at[slot], kv_scr,
                                           vt_scr, first_of_seq)
            mixer_pace = 2
        else:
            live = s < n_tiles

            @pl.when(first_of_seq & live)
            def _():
                c_ref[...] = jnp.zeros_like(c_ref)
                n_ref[...] = jnp.zeros_like(n_ref)
                m_ref[...] = jnp.zeros_like(m_ref)

            L = M_SCAN_CHUNK
            mixer_work = itertools.chain.from_iterable(
                _mlstm_chunk([dict(qkvo=qkvo_ref.at[pl.ds(lo, L)], gates=gates_ref.at[pl.ds(lo, L)],
                                   hg=a_scr.at[slot, pl.ds(lo, L)],
                                   c=c_ref.at[0], n=n_ref.at[0], m=m_ref.at[0])],
                             hn_ref, L, keep=live)
                for lo in range(0, x_ref.shape[0], L))
            mixer_pace = 1
    else:
        mixer_work, mixer_pace = iter(()), 0
        x = x_ref[...]

    xn = _rms(x, g_ref[...]).astype(BF16)
    for c in range(D_FF // FFN_CHUNK):
        lo = c * FFN_CHUNK
        gate = jnp.dot(xn, win_ref[:, lo:lo + FFN_CHUNK], preferred_element_type=F32)
        up = jnp.dot(xn, win_ref[:, D_FF + lo:D_FF + lo + FFN_CHUNK], preferred_element_type=F32)
        h_ref[:, lo:lo + FFN_CHUNK] = (gate * jax.nn.sigmoid(gate) * up).astype(BF16)
        for _ in range(mixer_pace):
            next(mixer_work, None)
    _run(mixer_work)
    y = x + FFN_RES * jnp.dot(h_ref[...], wout_ref[...], preferred_element_type=F32)
    if final:
        o_ref[...] = _rms(y, gf_ref[...])
        return
    o_ref[...] = y
    if proj:
        yn = _rms(y, gmix_ref[...]).astype(BF16)
        (_mlstm_in_proj if proj == "mlstm" else _swa_in_proj)(yn, *proj_refs)


def _ffn(x, g, w_in, w_out, g_final, *, layer, tm, final=False, attn=None, scan=None, proj=None):
    n = x.shape[0]
    n_tiles = n // tm
    n_steps = n_tiles
    tile = lambda width: pl.BlockSpec((tm, width), lambda i: (i, 0))
    scratch = [pltpu.VMEM((tm, D_FF), BF16)]
    body_attn = body_scan = None
    if attn or scan:
        n_steps = n_tiles + 1
        tile = lambda width: pl.BlockSpec((tm, width), lambda i: (jnp.maximum(i - 1, 0), 0))
        ahead = lambda width: pl.BlockSpec((tm, width), lambda i: (jnp.minimum(i, n_tiles - 1), 0))
        scratch.append(pltpu.VMEM((2, tm, D_MODEL), BF16))
    args, specs = [x], [tile(D_MODEL)]
    if attn:
        q, kv, sinks, w_mix, mix_layer, tiles_per_seq = attn
        body_attn = (n_tiles, tiles_per_seq)
        prev_blocks = tm // WINDOW
        prev = pl.BlockSpec(
            (WINDOW, 2 * A_KV), lambda i: (jnp.maximum(jnp.minimum(i, n_tiles - 1) * prev_blocks - 1, 0), 0))
        scratch += [pltpu.VMEM((WINDOW + tm, 2 * A_KV), BF16), pltpu.VMEM((A_KV, WINDOW + tm), BF16)]
        args += [q, kv, kv, sinks, w_mix]
        specs += [ahead(D_MODEL), prev, ahead(2 * A_KV), pl.BlockSpec(memory_space=pltpu.SMEM),
                  _layer_spec((D_MODEL, D_MODEL), mix_layer)]
    elif scan:
        qkvo, gates, hnorm, w_mix, mix_layer, tiles_per_seq = scan
        body_scan = (n_tiles, tiles_per_seq)
        args += [qkvo, gates, hnorm, w_mix]
        specs += [ahead(M_MAIN), ahead(LANES), _layer_spec((1, M_V), mix_layer),
                  _layer_spec((D_MODEL, D_MODEL), mix_layer)]
    args += [g, w_in, w_out, g_final]
    specs += [_layer_spec((1, D_MODEL), layer), _layer_spec((D_MODEL, 2 * D_FF), layer),
              _layer_spec((D_FF, D_MODEL), layer), _const_spec((1, D_MODEL))]
    out_specs = [tile(D_MODEL)]
    out_shape = [jax.ShapeDtypeStruct((n, D_MODEL), F32)]
    kind = None
    if proj:
        kind, gmix, norm_layer = proj[:3]
        args.append(gmix)
        specs.append(_layer_spec((1, D_MODEL), norm_layer))
        if kind == "mlstm":
            w, wg, bg, mix_layer = proj[3:]
            args += [w, wg, bg]
            specs += [_layer_spec((D_MODEL, M_MAIN), mix_layer), _layer_spec((D_MODEL, LANES), mix_layer),
                      _layer_spec((1, LANES), mix_layer)]
            widths = [(M_MAIN, BF16), (LANES, F32)]
        else:
            wq, wkv, cos, sin_signed, mix_layer = proj[3:]
            n_tab = cos.shape[0] // tm
            table = pl.BlockSpec((tm, LANES), lambda i: (i % n_tab, 0))
            args += [wq, wkv, cos, sin_signed]
            specs += [_layer_spec((D_MODEL, D_MODEL), mix_layer), _layer_spec((D_MODEL, 2 * A_KV), mix_layer),
                      table, table]
            widths = [(D_MODEL, BF16), (2 * A_KV, F32)]
        out_specs += [tile(wd) for wd, _ in widths]
        out_shape += [jax.ShapeDtypeStruct((n, wd), dt) for wd, dt in widths]
    if scan:
        n_seqs = n_tiles // tiles_per_seq
        for shp in [(M_HEADS, M_DK, M_DV), (M_HEADS, M_DK), (M_HEADS, LANES)]:
            out_specs.append(pl.BlockSpec(
                (1, *shp),
                lambda i, nd=len(shp): (jnp.minimum(i, n_tiles - 1) // tiles_per_seq,) + (0,) * nd))
            out_shape.append(jax.ShapeDtypeStruct((n_seqs, *shp), F32))
    return pl.pallas_call(
        functools.partial(_ffn_body, final=final, proj=kind, attn=body_attn, scan=body_scan),
        grid=(n_steps,),
        in_specs=specs,
        out_specs=out_specs,
        out_shape=out_shape,
        scratch_shapes=scratch,
        compiler_params=_params(1, 56),
        name="half_ffn",
    )(*args)


def _ffn_small_body(*refs, mixer_out, final, proj):
    refs = iter(refs)
    x_ref = next(refs)
    if mixer_out:
        a_ref, wmix_ref = next(refs), next(refs)
    g_ref, wg_ref, wu_ref, wo_ref, gf_ref = [next(refs) for _ in range(5)]
    if proj:
        gmix_ref = next(refs)
        proj_refs = [next(refs) for _ in range(3 if proj == "mlstm" else 4)]
    o_ref = next(refs)
    if proj:
        proj_refs += [next(refs), next(refs)]
    x_scr, xn_scr, acc_scr = next(refs), next(refs), next(refs)
    c = pl.program_id(0)

    @pl.when(c == 0)
    def _():
        if mixer_out:
            x = x_ref[...] + jnp.dot(a_ref[...], wmix_ref[...], preferred_element_type=F32)
        else:
            x = x_ref[...]
        x_scr[...] = x
        xn_scr[...] = _rms(x, g_ref[...]).astype(BF16)
        acc_scr[...] = jnp.zeros_like(acc_scr)

    xn = xn_scr[...]
    gate = jnp.dot(xn, wg_ref[...], preferred_element_type=F32)
    up = jnp.dot(xn, wu_ref[...], preferred_element_type=F32)
    h = (gate * jax.nn.sigmoid(gate) * up).astype(BF16)
    acc_scr[...] += jnp.dot(h, wo_ref[...], preferred_element_type=F32)

    @pl.when(c == pl.num_programs(0) - 1)
    def _():
        y = x_scr[...] + FFN_RES * acc_scr[...]
        if final:
            o_ref[...] = _rms(y, gf_ref[...])
            return
        o_ref[...] = y
        if proj:
            yn = _rms(y, gmix_ref[...]).astype(BF16)
            (_mlstm_in_proj if proj == "mlstm" else _swa_in_proj)(yn, *proj_refs)


def _ffn_small(x, g, w_in, w_out, g_final, *, layer, final=False, mixer_out=None, proj=None):
    n = x.shape[0]
    half = D_FF // FFN_SMALL_SPLIT
    whole = lambda width: pl.BlockSpec((n, width), lambda c: (0, 0))
    args, specs = [x], [whole(D_MODEL)]
    if mixer_out:
        a, w_mix, mix_layer = mixer_out
        args += [a, w_mix]
        specs += [whole(D_MODEL), _layer_spec((D_MODEL, D_MODEL), mix_layer)]
    args += [g, w_in, w_in, w_out, g_final]
    specs += [_layer_spec((1, D_MODEL), layer),
              pl.BlockSpec((None, D_MODEL, half), lambda c: (layer, 0, c)),
              pl.BlockSpec((None, D_MODEL, half), lambda c: (layer, 0, FFN_SMALL_SPLIT + c)),
              pl.BlockSpec((None, half, D_MODEL), lambda c: (layer, c, 0)),
              _const_spec((1, D_MODEL))]
    out_specs = [whole(D_MODEL)]
    out_shape = [jax.ShapeDtypeStruct((n, D_MODEL), F32)]
    kind = None
    if proj:
        kind, gmix, norm_layer = proj[:3]
        args.append(gmix)
        specs.append(_layer_spec((1, D_MODEL), norm_layer))
        if kind == "mlstm":
            w, wg, bg, mix_layer = proj[3:]
            args += [w, wg, bg]
            specs += [_layer_spec((D_MODEL, M_MAIN), mix_layer), _layer_spec((D_MODEL, LANES), mix_layer),
                      _layer_spec((1, LANES), mix_layer)]
            widths = [(M_MAIN, BF16), (LANES, F32)]
        else:
            wq, wkv, cos, sin_signed, mix_layer = proj[3:]
            args += [wq, wkv, cos, sin_signed]
            specs += [_layer_spec((D_MODEL, D_MODEL), mix_layer), _layer_spec((D_MODEL, 2 * A_KV), mix_layer),
                      whole(LANES), whole(LANES)]
            widths = [(D_MODEL, BF16), (2 * A_KV, F32)]
        out_specs += [whole(wd) for wd, _ in widths]
        out_shape += [jax.ShapeDtypeStruct((n, wd), dt) for wd, dt in widths]
    return pl.pallas_call(
        functools.partial(_ffn_small_body, mixer_out=bool(mixer_out), final=final, proj=kind),
        grid=(FFN_SMALL_SPLIT,),
        in_specs=specs,
        out_specs=out_specs,
        out_shape=out_shape,
        scratch_shapes=[pltpu.VMEM((n, D_MODEL), F32), pltpu.VMEM((n, D_MODEL), BF16),
                        pltpu.VMEM((n, D_MODEL), F32)],
        compiler_params=_params(1, 40),
        name="half_ffn_small",
    )(*args)


def _mlstm_body(qkvo_ref, gates_ref, c0_ref, n0_ref, m0_ref, hn_ref,
                hg_ref, c_ref, n_ref, m_ref, *, bg, L):
    @pl.when(pl.program_id(1) == 0)
    def _():
        c_ref[...] = c0_ref[...]
        n_ref[...] = n0_ref[...]
        m_ref[...] = m0_ref[...]

    _run(_mlstm_chunk([dict(qkvo=qkvo_ref.at[b], gates=gates_ref.at[b], hg=hg_ref.at[b],
                            c=c_ref.at[b], n=n_ref.at[b], m=m_ref.at[b]) for b in range(bg)],
                      hn_ref, L))


def _mlstm_chunk(seqs, hn_ref, L, keep=None):
    scale = M_DK ** -0.5
    row = lax.broadcasted_iota(jnp.int32, (L, L), 0)
    col = lax.broadcasted_iota(jnp.int32, (L, L), 1)
    causal = col <= row
    tril = causal.astype(BF16)
    lane = lax.broadcasted_iota(jnp.int32, (L, LANES), 1)

    probs = []
    for seq in seqs:
        gts = seq["gates"][...]
        lf = jax.nn.log_sigmoid(gts)
        lf_hi = lf.astype(BF16)
        r1 = lf - lf_hi.astype(F32)
        lf_mid = r1.astype(BF16)
        lf_lo = (r1 - lf_mid.astype(F32)).astype(BF16)
        bcs = (jnp.dot(tril, lf_hi, preferred_element_type=F32)
               + jnp.dot(tril, lf_mid, preferred_element_type=F32)
               + jnp.dot(tril, lf_lo, preferred_element_type=F32))
        both = jnp.where(lane < M_HEADS, gts, bcs)
        if L % LANES:
            both = jnp.concatenate([both, jnp.zeros((LANES - L, LANES), F32)], axis=0)
        both_t = both.T
        for h in range(M_HEADS):
            p = dict(seq=seq, h=h)
            b_row = both_t[M_HEADS + h:M_HEADS + h + 1, 0:L]
            p["b_last"] = b_row[:, L - 1:L]
            p["bmi_row"] = b_row - both_t[h:h + 1, 0:L]
            b_col = bcs[:, M_HEADS + h:M_HEADS + h + 1]
            p["b_rep"] = jnp.broadcast_to(b_col, (L, LANES))
            p["c_rep"] = jnp.broadcast_to(gts[:, h:h + 1] - b_col, (L, LANES))
            qkvo_ref = seq["qkvo"]
            p["m_prev"] = seq["m"][h:h + 1, 0:1]
            p["n_row"] = seq["n"][h:h + 1, :]
            p["c_prev"] = seq["c"][h]
            p["q"] = qkvo_ref[:, h * M_DK:(h + 1) * M_DK]
            p["k"] = qkvo_ref[:, M_QK + h * M_DK:M_QK + (h + 1) * M_DK]
            p["v"] = qkvo_ref[:, 2 * M_QK + h * M_DV:2 * M_QK + (h + 1) * M_DV]
            probs.append(p)

    def wide(c, width):
        return c[:, :width] if width <= LANES else jnp.concatenate([c] * (width // LANES), axis=1)

    def fold(x):
        acc = x[:, :LANES]
        for t in range(1, x.shape[1] // LANES):
            acc = acc + x[:, t * LANES:(t + 1) * LANES]
        return acc

    yield
    for p in probs:
        p["s"] = lax.dot_general(p["q"], p["k"], (((1,), (1,)), ((), ())), preferred_element_type=F32)
        p["qc"] = jnp.dot(p["q"], p["c_prev"].astype(BF16), preferred_element_type=F32)
        n_rep = jnp.broadcast_to(p["n_row"], (LANES, M_DK)).astype(BF16)
        p["qn"] = lax.dot_general(p["q"], n_rep, (((1,), (1,)), ((), ())), preferred_element_type=F32)

    yield
    for p in probs:
        d = jnp.where(causal, wide(p["b_rep"], L) - p["bmi_row"], -jnp.inf)
        g_rep = p["b_rep"] + p["m_prev"]
        m_t = jnp.maximum(g_rep, jnp.max(d, axis=-1, keepdims=True))
        w = jnp.exp(d - wide(m_t, L)) * (p["s"] * scale)
        p["m_t"] = m_t
        p["inter"] = jnp.exp(g_rep - m_t)
        p["w_sum"] = jnp.sum(fold(w), axis=-1, keepdims=True)
        p["w"] = w.astype(BF16)
        b_last = p["b_last"]
        a_row = b_last - p["bmi_row"]
        m_new = jnp.maximum(p["m_prev"] + b_last, jnp.max(a_row, axis=-1, keepdims=True))
        p["m_new"] = m_new
        p["decay"] = jnp.exp(p["m_prev"] + b_last - m_new)
        p["wk"] = (jnp.exp(b_last + p["c_rep"] - m_new) * scale) * p["k"].astype(F32)

    yield
    for p in probs:
        p["wv"] = jnp.dot(p["w"], p["v"], preferred_element_type=F32)
        p["kv"] = lax.dot_general(p["wk"].astype(BF16), p["v"], (((0,), (0,)), ((), ())),
                                  preferred_element_type=F32)

    def updated(new, old):
        return new if keep is None else jnp.where(keep, new, old)

    yield
    for p in probs:
        seq, h = p["seq"], p["h"]
        num = wide(p["inter"], M_DV) * p["qc"] + p["wv"]
        den = p["inter"] * p["qn"] + p["w_sum"]
        hh = num / wide(jnp.maximum(jnp.abs(den), jnp.exp(-p["m_t"])), M_DV)
        hh = hh * lax.rsqrt(jnp.mean(hh * hh, axis=-1, keepdims=True) + EPS)
        hh = hh * hn_ref[:, h * M_DV:(h + 1) * M_DV]
        og = seq["qkvo"][:, 2 * M_QK + M_V + h * M_DV:2 * M_QK + M_V + (h + 1) * M_DV]
        seq["hg"][:, h * M_DV:(h + 1) * M_DV] = (jax.nn.sigmoid(og.astype(F32)) * hh).astype(BF16)
        seq["c"][h] = updated(p["decay"] * p["c_prev"] + p["kv"], p["c_prev"])
        seq["n"][h:h + 1, :] = updated(
            p["decay"] * p["n_row"] + jnp.sum(p["wk"], axis=0, keepdims=True), p["n_row"])
        m_row = jnp.broadcast_to(p["m_new"], (1, LANES))
        seq["m"][h:h + 1, :] = updated(m_row, jnp.broadcast_to(p["m_prev"], (1, LANES)))


def _mlstm(qkvo, gates, c0, n0, m0, hnorm, *, layer, bg, L):
    nb, t, _ = qkvo.shape
    state_shapes = [(bg, M_HEADS, M_DK, M_DV), (bg, M_HEADS, M_DK), (bg, M_HEADS, LANES)]
    state_specs = [pl.BlockSpec(shp, lambda g, c, nd=len(shp): (g,) + (0,) * (nd - 1))
                   for shp in state_shapes]
    init_specs = [pl.BlockSpec((None, *shp), lambda g, c, nd=len(shp): (layer, g) + (0,) * (nd - 1))
                  for shp in state_shapes]
    return pl.pallas_call(
        functools.partial(_mlstm_body, bg=bg, L=L),
        grid=(nb // bg, t // L),
        in_specs=[
            pl.BlockSpec((bg, L, M_MAIN), lambda g, c: (g, c, 0)),
            pl.BlockSpec((bg, L, LANES), lambda g, c: (g, c, 0)),
            *init_specs,
            _layer_spec((1, M_V), layer),
        ],
        out_specs=[pl.BlockSpec((bg, L, M_V), lambda g, c: (g, c, 0)), *state_specs],
        out_shape=[
            jax.ShapeDtypeStruct((nb, t, M_V), BF16),
            jax.ShapeDtypeStruct((nb, M_HEADS, M_DK, M_DV), F32),
            jax.ShapeDtypeStruct((nb, M_HEADS, M_DK), F32),
            jax.ShapeDtypeStruct((nb, M_HEADS, LANES), F32),
        ],
        compiler_params=_params(2, 32),
        name="mlstm_scan",
    )(qkvo, gates, c0, n0, m0, hnorm)


def _attn_groups(sinks_ref, q_ref, o_ref, kv_scr, vt_scr, problems):
    per_slice = 8

    def pace(idx):
        return (idx + 1) % per_slice == 0

    scores = []
    for idx, (q_lo, q_rows, k_lo, k_rows, g, first_valid) in enumerate(problems):
        kb = kv_scr[k_lo:k_lo + k_rows, g * A_HD:(g + 1) * A_HD]
        qs = jnp.concatenate(
            [q_ref[q_lo:q_lo + q_rows, hd * A_HD:(hd + 1) * A_HD]
             for hd in range(A_GROUP * g, A_GROUP * (g + 1))], axis=0)
        scores.append(lax.dot_general(kb, qs, (((1,), (1,)), ((), ())), preferred_element_type=F32))
        if pace(idx):
            yield
    probs = []
    for idx, (s, (q_lo, q_rows, k_lo, k_rows, g, first_valid)) in enumerate(zip(scores, problems)):
        if first_valid is not None:
            key = lax.broadcasted_iota(jnp.int32, s.shape, 0)
            s = jnp.where(key >= first_valid, s, -jnp.inf)
        sk = jnp.concatenate([jnp.full((1, q_rows), sinks_ref[hd], F32)
                              for hd in range(A_GROUP * g, A_GROUP * (g + 1))], axis=1)
        m = jnp.maximum(jnp.max(s, axis=0, keepdims=True), sk)
        p = jnp.exp(s - m)
        den = jnp.sum(p, axis=0, keepdims=True) + jnp.exp(sk - m)
        probs.append((p.astype(BF16), den))
        if pace(idx):
            yield
    outs = []
    for idx, ((p, den), (q_lo, q_rows, k_lo, k_rows, g, first_valid)) in enumerate(zip(probs, problems)):
        if vt_scr is None:
            vb = kv_scr[k_lo:k_lo + k_rows, A_KV + g * A_HD:A_KV + (g + 1) * A_HD]
            o_t = lax.dot_general(vb, p, (((0,), (0,)), ((), ())), preferred_element_type=F32)
        else:
            o_t = jnp.dot(vt_scr[g * A_HD:(g + 1) * A_HD, k_lo:k_lo + k_rows], p,
                          preferred_element_type=F32)
        outs.append(o_t / den)
        if pace(idx):
            yield
    for idx, (o_t, (q_lo, q_rows, k_lo, k_rows, g, first_valid)) in enumerate(zip(outs, problems)):
        o = o_t.T
        for j in range(A_GROUP):
            hd = A_GROUP * g + j
            o_ref[q_lo:q_lo + q_rows, hd * A_HD:(hd + 1) * A_HD] = (
                o[j * q_rows:(j + 1) * q_rows].astype(BF16))
        if pace(idx):
            yield


def _attn_prompt_tile(sinks_ref, q_ref, kvp_ref, kvo_ref, o_ref, kv_scr, vt_scr, first_of_seq):
    rows = q_ref.shape[0]
    kv_scr[0:WINDOW, :] = kvp_ref[...].astype(BF16)
    kv_scr[WINDOW:WINDOW + rows, :] = kvo_ref[...].astype(BF16)
    vt_scr[:, 0:WINDOW] = kvp_ref[:, A_KV:2 * A_KV].T.astype(BF16)
    vt_scr[:, WINDOW:WINDOW + rows] = kvo_ref[:, A_KV:2 * A_KV].T.astype(BF16)
    band = WINDOW + CHUNK
    n_missing = jnp.where(first_of_seq, WINDOW, 0)
    problems = []
    for i in range(rows // CHUNK):
        first_valid = (n_missing - i * CHUNK) if i * CHUNK < WINDOW else None
        for g in range(A_KV_HEADS):
            problems.append((i * CHUNK, CHUNK, i * CHUNK, band, g, first_valid))
    yield
    yield from _attn_groups(sinks_ref, q_ref, o_ref, kv_scr, vt_scr, problems)


def _attn_sample_body(sinks_ref, q_ref, ck_ref, cv_ref, kv_ref, o_ref, kv_scr):
    rows = ck_ref.shape[1]
    t = q_ref.shape[1]
    kv_scr[0:rows, 0:A_KV] = ck_ref[0].astype(BF16)
    kv_scr[0:rows, A_KV:2 * A_KV] = cv_ref[0].astype(BF16)
    kv_scr[rows:rows + t, :] = kv_ref[0].astype(BF16)
    _run(_attn_groups(sinks_ref, q_ref.at[0], o_ref.at[0], kv_scr, None,
                      [(0, t, 0, rows + t, g, None) for g in range(A_KV_HEADS)]))


def _attn_sample(q, cache_k, cache_v, kv, sinks, *, layer):
    nb, t, _ = q.shape
    rows = cache_k.shape[2]
    return pl.pallas_call(
        _attn_sample_body,
        grid=(nb,),
        in_specs=[
            pl.BlockSpec(memory_space=pltpu.SMEM),
            pl.BlockSpec((1, t, D_MODEL), lambda b: (b, 0, 0)),
            pl.BlockSpec((None, 1, rows, A_KV), lambda b: (layer, b, 0, 0)),
            pl.BlockSpec((None, 1, rows, A_KV), lambda b: (layer, b, 0, 0)),
            pl.BlockSpec((1, t, 2 * A_KV), lambda b: (b, 0, 0)),
        ],
        out_specs=pl.BlockSpec((1, t, D_MODEL), lambda b: (b, 0, 0)),
        out_shape=jax.ShapeDtypeStruct((nb, t, D_MODEL), BF16),
        scratch_shapes=[pltpu.VMEM((rows + t, 2 * A_KV), BF16)],
        compiler_params=_params(1, 32),
        name="swa_sample_attn",
    )(sinks, q, cache_k, cache_v, kv)


def _rope_tables(pos):
    inv = ROPE_THETA ** (-jnp.arange(0, A_HD, 2, dtype=F32) / A_HD)
    ang = pos.astype(F32)[:, None] * inv[None, :]
    cos = jnp.cos(ang)
    sin = jnp.sin(ang)
    reps = LANES // A_HD
    return (jnp.concatenate([cos, cos] * reps, axis=-1),
            jnp.concatenate([-sin, sin] * reps, axis=-1))


def kernel(x_prompt, x_sample, state_mlstm_C, state_mlstm_n, state_mlstm_m, cache_swa_k, cache_swa_v,
           ffn_norm1, ffn_w_in1, ffn_w_out1, mix_norm, mlstm_w_in, mlstm_b_gates, mlstm_head_norm,
           mlstm_w_out, swa_w_qkv, swa_sinks, swa_w_out, ffn_norm2, ffn_w_in2, ffn_w_out2, final_norm):
    bp, tp, _ = x_prompt.shape
    bs, ts, _ = x_sample.shape
    tm_p = 512

    w_in1, w_out1 = ffn_w_in1.astype(BF16), ffn_w_out1.astype(BF16)
    w_in2, w_out2 = ffn_w_in2.astype(BF16), ffn_w_out2.astype(BF16)
    m_w_main = mlstm_w_in[:, :, :M_MAIN].astype(BF16)
    n_gates = 2 * M_HEADS
    m_w_gates = jnp.pad(mlstm_w_in[:, :, M_MAIN:], ((0, 0), (0, 0), (0, LANES - n_gates))).astype(BF16)
    m_b_gates = jnp.pad(mlstm_b_gates.astype(F32), ((0, 0), (0, LANES - n_gates)))[:, None, :]
    m_w_out = mlstm_w_out.astype(BF16)
    s_wq = swa_w_qkv[:, :, :D_MODEL].astype(BF16)
    s_wkv = swa_w_qkv[:, :, D_MODEL:].astype(BF16)
    s_w_out = swa_w_out.astype(BF16)
    sinks = swa_sinks.astype(F32)
    rows3 = lambda a: a.astype(F32)[:, None, :]
    g1, g2, gmix, hnorm = rows3(ffn_norm1), rows3(ffn_norm2), rows3(mix_norm), rows3(mlstm_head_norm)
    gf = final_norm.astype(F32)[None, :]

    cos_p, sin_p = _rope_tables(jnp.arange(tp))
    cos_s, sin_s = _rope_tables(PAST_LEN + jnp.arange(ts))
    cos_s, sin_s = jnp.tile(cos_s, (bs, 1)), jnp.tile(sin_s, (bs, 1))

    yp = x_prompt.reshape(bp * tp, D_MODEL)
    ys = x_sample.reshape(bs * ts, D_MODEL)
    c0_s = state_mlstm_C.astype(F32)
    n0_s = state_mlstm_n.astype(F32)
    m0_s = jnp.broadcast_to(state_mlstm_m.astype(F32)[..., None], state_mlstm_m.shape + (LANES,))
    rows = cache_swa_k.shape[2]
    ck = cache_swa_k.astype(F32).reshape(-1, bs, rows, A_KV)
    cv = cache_swa_v.astype(F32).reshape(-1, bs, rows, A_KV)

    p_c, p_n, p_m, p_k, p_v = [], [], [], [], []
    s_c, s_n, s_m, s_k, s_v = [], [], [], [], []
    for i in range(DEPTH):
        j = i // 2
        if i % 2 == 0:
            proj_p = proj_s = ("mlstm", gmix, i, m_w_main, m_w_gates, m_b_gates, j)
        else:
            proj_p = ("swa", gmix, i, s_wq, s_wkv, cos_p, sin_p, j)
            proj_s = ("swa", gmix, i, s_wq, s_wkv, cos_s, sin_s, j)
        yp, pa_p, pb_p = _ffn(yp, g1, w_in1, w_out1, gf, layer=i, tm=tm_p, proj=proj_p)
        ys, pa_s, pb_s = _ffn_small(ys, g1, w_in1, w_out1, gf, layer=i, proj=proj_s)
        if i % 2 == 0:
            mix_p = dict(scan=(pa_p, pb_p, hnorm, m_w_out, j, tp // tm_p))

            qkvo, gates = pa_s, pb_s
            hg, c, n, m = _mlstm(qkvo.reshape(bs, ts, M_MAIN), gates.reshape(bs, ts, LANES),
                                 c0_s, n0_s, m0_s, hnorm, layer=j, bg=bs,
                                 L=min(CHUNK, ts))
            mix_s = (hg.reshape(bs * ts, M_V), m_w_out, j)
            s_c.append(c); s_n.append(n); s_m.append(m[:, :, 0])
        else:
            q, kv = pa_p, pb_p
            kv3 = kv.reshape(bp, tp, 2 * A_KV)
            mix_p = dict(attn=(q, kv, sinks[j], s_w_out, j, tp // tm_p))
            keep = min(WINDOW, tp)
            p_k.append(kv3[:, tp - keep:, :A_KV].reshape(bp, keep, A_KV_HEADS, A_HD))
            p_v.append(kv3[:, tp - keep:, A_KV:].reshape(bp, keep, A_KV_HEADS, A_HD))

            q, kv = pa_s, pb_s
            kv3 = kv.reshape(bs, ts, 2 * A_KV)
            o = _attn_sample(q.reshape(bs, ts, D_MODEL), ck, cv, kv3, sinks[j], layer=j)
            mix_s = (o.reshape(bs * ts, D_MODEL), s_w_out, j)
            s_k.append(kv3[:, :, :A_KV].reshape(bs, ts, A_KV_HEADS, A_HD))
            s_v.append(kv3[:, :, A_KV:].reshape(bs, ts, A_KV_HEADS, A_HD))
        last = i == DEPTH - 1
        yp, *state = _ffn(yp, g2, w_in2, w_out2, gf, layer=i, tm=tm_p, final=last, **mix_p)
        if state:
            c, n, m = state
            p_c.append(c); p_n.append(n); p_m.append(m[:, :, 0])
        ys, = _ffn_small(ys, g2, w_in2, w_out2, gf, layer=i, final=last, mixer_out=mix_s)

    return (yp.reshape(bp, tp, D_MODEL), ys.reshape(bs, ts, D_MODEL),
            jnp.stack(p_c), jnp.stack(p_n), jnp.stack(p_m), jnp.stack(p_k), jnp.stack(p_v),
            jnp.stack(s_c), jnp.stack(s_n), jnp.stack(s_m), jnp.stack(s_k), jnp.stack(s_v))
```

```python
import functools
import itertools

import jax
import jax.numpy as jnp
from jax import lax
from jax.experimental import pallas as pl
from jax.experimental.pallas import tpu as pltpu

F32 = jnp.float32
BF16 = jnp.bfloat16

D_MODEL = 1024
DEPTH = 4
CHUNK = 64
M_HEADS = 4
M_DK = D_MODEL // 8
M_DV = D_MODEL // M_HEADS
M_QK = M_HEADS * M_DK
M_V = M_HEADS * M_DV
M_MAIN = 2 * M_QK + 2 * M_V
A_HEADS = 16
A_KV_HEADS = 4
A_HD = D_MODEL // A_HEADS
A_GROUP = A_HEADS // A_KV_HEADS
A_KV = A_KV_HEADS * A_HD
WINDOW = 128
PAST_LEN = 4096
ROPE_THETA = 10000.0
D_FF = 11 * D_MODEL // 4
FFN_RES = 0.5
EPS = 1e-6

LANES = 128
FFN_CHUNK = 256
PROJ_CHUNK = 512
FFN_SMALL_SPLIT = 2
M_SCAN_CHUNK = 256
MIB = 1024 * 1024


def _rms(x, g):
    return x * lax.rsqrt(jnp.mean(x * x, axis=-1, keepdims=True) + EPS) * g


def _const_spec(shape):
    nd = len(shape)
    return pl.BlockSpec(shape, lambda *_: (0,) * nd, pipeline_mode=pl.Buffered(1))


def _layer_spec(shape, layer):
    nd = len(shape)
    return pl.BlockSpec((None, *shape), lambda *_: (layer,) + (0,) * nd, pipeline_mode=pl.Buffered(1))


def _run(work):
    for _ in work:
        pass


def _params(n_grid, vmem_mib, independent=False):
    return pltpu.CompilerParams(
        dimension_semantics=("parallel" if independent else "arbitrary",) * n_grid,
        vmem_limit_bytes=vmem_mib * MIB)


def _mlstm_in_proj(xn, w_ref, wg_ref, bg_ref, o_ref, og_ref):
    for c in range(M_MAIN // PROJ_CHUNK):
        lo = c * PROJ_CHUNK
        o_ref[:, lo:lo + PROJ_CHUNK] = jnp.dot(
            xn, w_ref[:, lo:lo + PROJ_CHUNK], preferred_element_type=F32).astype(BF16)
    og_ref[...] = jnp.dot(xn, wg_ref[...], preferred_element_type=F32) + bg_ref[...]


def _rope(x, cos, sin_signed, first_half):
    swapped = jnp.where(first_half, pltpu.roll(x, LANES - A_HD // 2, 1), pltpu.roll(x, A_HD // 2, 1))
    return x * cos + swapped * sin_signed


def _swa_in_proj(xn, wq_ref, wkv_ref, cos_ref, sin_ref, q_ref, kv_ref):
    cos = cos_ref[...]
    sin_signed = sin_ref[...]
    lane = lax.broadcasted_iota(jnp.int32, cos.shape, 1)
    first_half = (lane & (A_HD // 2)) == 0
    q_scale = A_HD ** -0.5
    for c in range(D_MODEL // PROJ_CHUNK):
        q = jnp.dot(xn, wq_ref[:, c * PROJ_CHUNK:(c + 1) * PROJ_CHUNK], preferred_element_type=F32)
        for j in range(PROJ_CHUNK // LANES):
            lo = c * PROJ_CHUNK + j * LANES
            blk = _rope(q[:, j * LANES:(j + 1) * LANES], cos, sin_signed, first_half)
            q_ref[:, lo:lo + LANES] = (blk * q_scale).astype(BF16)
    kv = jnp.dot(xn, wkv_ref[...], preferred_element_type=F32)
    for j in range(A_KV // LANES):
        kv_ref[:, j * LANES:(j + 1) * LANES] = _rope(
            kv[:, j * LANES:(j + 1) * LANES], cos, sin_signed, first_half)
    kv_ref[:, A_KV:2 * A_KV] = kv[:, A_KV:2 * A_KV]


def _ffn_body(*refs, final, proj, attn, scan):
    refs = iter(refs)
    x_ref = next(refs)
    if attn:
        q_ref, kvp_ref, kvo_ref, sinks_ref, wmix_ref = [next(refs) for _ in range(5)]
    elif scan:
        qkvo_ref, gates_ref, hn_ref, wmix_ref = [next(refs) for _ in range(4)]
    g_ref, win_ref, wout_ref, gf_ref = next(refs), next(refs), next(refs), next(refs)
    if proj:
        gmix_ref = next(refs)
        proj_refs = [next(refs) for _ in range(3 if proj == "mlstm" else 4)]
    o_ref = next(refs)
    if proj:
        proj_refs += [next(refs), next(refs)]
    if scan:
        c_ref, n_ref, m_ref = next(refs), next(refs), next(refs)
    h_ref = next(refs)

    if attn or scan:
        n_tiles, tiles_per_seq = attn or scan
        a_scr = next(refs)
        s = pl.program_id(0)

        @pl.when(s == 0)
        def _():
            a_scr[...] = jnp.zeros_like(a_scr)

        slot = lax.rem(s, 2)
        a = a_scr[1 - slot]
        x = x_ref[...] + jnp.dot(a, wmix_ref[...], preferred_element_type=F32)
        first_of_seq = lax.rem(jnp.minimum(s, n_tiles - 1), tiles_per_seq) == 0
        if attn:
            kv_scr, vt_scr = next(refs), next(refs)
            mixer_work = _attn_prompt_tile(sinks_ref, q_ref, kvp_ref, kvo_ref, a_scr.at[slot], kv_scr,
                                           vt_scr, first_of_seq)
            mixer_pace = 2
        else:
            live = s < n_tiles

            @pl.when(first_of_seq & live)
            def _():
                c_ref[...] = jnp.zeros_like(c_ref)
                n_ref[...] = jnp.zeros_like(n_ref)
                m_ref[...] = jnp.zeros_like(m_ref)

            L = M_SCAN_CHUNK
            mixer_work = itertools.chain.from_iterable(
                _mlstm_chunk([dict(qkvo=qkvo_ref.at[pl.ds(lo, L)], gates=gates_ref.at[pl.ds(lo, L)],
                                   hg=a_scr.at[slot, pl.ds(lo, L)],
                                   c=c_ref.at[0], n=n_ref.at[0], m=m_ref.at[0])],
                             hn_ref, L, keep=live)
                for lo in range(0, x_ref.shape[0], L))
            mixer_pace = 1
    else:
        mixer_work, mixer_pace = iter(()), 0
        x = x_ref[...]

    xn = _rms(x, g_ref[...]).astype(BF16)
    for c in range(D_FF // FFN_CHUNK):
        lo = c * FFN_CHUNK
        gate = jnp.dot(xn, win_ref[:, lo:lo + FFN_CHUNK], preferred_element_type=F32)
        up = jnp.dot(xn, win_ref[:, D_FF + lo:D_FF + lo + FFN_CHUNK], preferred_element_type=F32)
        h_ref[:, lo:lo + FFN_CHUNK] = (gate * jax.nn.sigmoid(gate) * up).astype(BF16)
        for _ in range(mixer_pace):
            next(mixer_work, None)
    _run(mixer_work)
    y = x + FFN_RES * jnp.dot(h_ref[...], wout_ref[...], preferred_element_type=F32)
    if final:
        o_ref[...] = _rms(y, gf_ref[...])
        return
    o_ref[...] = y
    if proj:
        yn = _rms(y, gmix_ref[...]).astype(BF16)
        (_mlstm_in_proj if proj == "mlstm" else _swa_in_proj)(yn, *proj_refs)


def _ffn(x, g, w_in, w_out, g_final, *, layer, tm, final=False, attn=None, scan=None, proj=None):
    n = x.shape[0]
    n_tiles = n // tm
    n_steps = n_tiles
    tile = lambda width: pl.BlockSpec((tm, width), lambda i: (i, 0))
    scratch = [pltpu.VMEM((tm, D_FF), BF16)]
    body_attn = body_scan = None
    if attn or scan:
        n_steps = n_tiles + 1
        tile = lambda width: pl.BlockSpec((tm, width), lambda i: (jnp.maximum(i - 1, 0), 0))
        ahead = lambda width: pl.BlockSpec((tm, width), lambda i: (jnp.minimum(i, n_tiles - 1), 0))
        scratch.append(pltpu.VMEM((2, tm, D_MODEL), BF16))
    args, specs = [x], [tile(D_MODEL)]
    if attn:
        q, kv, sinks, w_mix, mix_layer, tiles_per_seq = attn
        body_attn = (n_tiles, tiles_per_seq)
        prev_blocks = tm // WINDOW
        prev = pl.BlockSpec(
            (WINDOW, 2 * A_KV), lambda i: (jnp.maximum(jnp.minimum(i, n_tiles - 1) * prev_blocks - 1, 0), 0))
        scratch += [pltpu.VMEM((WINDOW + tm, 2 * A_KV), BF16), pltpu.VMEM((A_KV, WINDOW + tm), BF16)]
        args += [q, kv, kv, sinks, w_mix]
        specs += [ahead(D_MODEL), prev, ahead(2 * A_KV), pl.BlockSpec(memory_space=pltpu.SMEM),
                  _layer_spec((D_MODEL, D_MODEL), mix_layer)]
    elif scan:
        qkvo, gates, hnorm, w_mix, mix_layer, tiles_per_seq = scan
        body_scan = (n_tiles, tiles_per_seq)
        args += [qkvo, gates, hnorm, w_mix]
        specs += [ahead(M_MAIN), ahead(LANES), _layer_spec((1, M_V), mix_layer),
                  _layer_spec((D_MODEL, D_MODEL), mix_layer)]
    args += [g, w_in, w_out, g_final]
    specs += [_layer_spec((1, D_MODEL), layer), _layer_spec((D_MODEL, 2 * D_FF), layer),
              _layer_spec((D_FF, D_MODEL), layer), _const_spec((1, D_MODEL))]
    out_specs = [tile(D_MODEL)]
    out_shape = [jax.ShapeDtypeStruct((n, D_MODEL), F32)]
    kind = None
    if proj:
        kind, gmix, norm_layer = proj[:3]
        args.append(gmix)
        specs.append(_layer_spec((1, D_MODEL), norm_layer))
        if kind == "mlstm":
            w, wg, bg, mix_layer = proj[3:]
            args += [w, wg, bg]
            specs += [_layer_spec((D_MODEL, M_MAIN), mix_layer), _layer_spec((D_MODEL, LANES), mix_layer),
                      _layer_spec((1, LANES), mix_layer)]
            widths = [(M_MAIN, BF16), (LANES, F32)]
        else:
            wq, wkv, cos, sin_signed, mix_layer = proj[3:]
            n_tab = cos.shape[0] // tm
            table = pl.BlockSpec((tm, LANES), lambda i: (i % n_tab, 0))
            args += [wq, wkv, cos, sin_signed]
            specs += [_layer_spec((D_MODEL, D_MODEL), mix_layer), _layer_spec((D_MODEL, 2 * A_KV), mix_layer),
                      table, table]
            widths = [(D_MODEL, BF16), (2 * A_KV, F32)]
        out_specs += [tile(wd) for wd, _ in widths]
        out_shape += [jax.ShapeDtypeStruct((n, wd), dt) for wd, dt in widths]
    if scan:
        n_seqs = n_tiles // tiles_per_seq
        for shp in [(M_HEADS, M_DK, M_DV), (M_HEADS, M_DK), (M_HEADS, LANES)]:
            out_specs.append(pl.BlockSpec(
                (1, *shp),
                lambda i, nd=len(shp): (jnp.minimum(i, n_tiles - 1) // tiles_per_seq,) + (0,) * nd))
            out_shape.append(jax.ShapeDtypeStruct((n_seqs, *shp), F32))
    return pl.pallas_call(
        functools.partial(_ffn_body, final=final, proj=kind, attn=body_attn, scan=body_scan),
        grid=(n_steps,),
        in_specs=specs,
        out_specs=out_specs,
        out_shape=out_shape,
        scratch_shapes=scratch,
        compiler_params=_params(1, 56, independent=not (attn or scan)),
        name="half_ffn",
    )(*args)


def _ffn_small_body(*refs, mixer_out, final, proj):
    refs = iter(refs)
    x_ref = next(refs)
    if mixer_out:
        a_ref, wmix_ref = next(refs), next(refs)
    g_ref, wg_ref, wu_ref, wo_ref, gf_ref = [next(refs) for _ in range(5)]
    if proj:
        gmix_ref = next(refs)
        proj_refs = [next(refs) for _ in range(3 if proj == "mlstm" else 4)]
    o_ref = next(refs)
    if proj:
        proj_refs += [next(refs), next(refs)]
    x_scr, xn_scr, acc_scr = next(refs), next(refs), next(refs)
    c = pl.program_id(0)

    @pl.when(c == 0)
    def _():
        if mixer_out:
            x = x_ref[...] + jnp.dot(a_ref[...], wmix_ref[...], preferred_element_type=F32)
        else:
            x = x_ref[...]
        x_scr[...] = x
        xn_scr[...] = _rms(x, g_ref[...]).astype(BF16)
        acc_scr[...] = jnp.zeros_like(acc_scr)

    xn = xn_scr[...]
    gate = jnp.dot(xn, wg_ref[...], preferred_element_type=F32)
    up = jnp.dot(xn, wu_ref[...], preferred_element_type=F32)
    h = (gate * jax.nn.sigmoid(gate) * up).astype(BF16)
    acc_scr[...] += jnp.dot(h, wo_ref[...], preferred_element_type=F32)

    @pl.when(c == pl.num_programs(0) - 1)
    def _():
        y = x_scr[...] + FFN_RES * acc_scr[...]
        if final:
            o_ref[...] = _rms(y, gf_ref[...])
            return
        o_ref[...] = y
        if proj:
            yn = _rms(y, gmix_ref[...]).astype(BF16)
            (_mlstm_in_proj if proj == "mlstm" else _swa_in_proj)(yn, *proj_refs)


def _ffn_small(x, g, w_in, w_out, g_final, *, layer, final=False, mixer_out=None, proj=None):
    n = x.shape[0]
    half = D_FF // FFN_SMALL_SPLIT
    whole = lambda width: pl.BlockSpec((n, width), lambda c: (0, 0))
    args, specs = [x], [whole(D_MODEL)]
    if mixer_out:
        a, w_mix, mix_layer = mixer_out
        args += [a, w_mix]
        specs += [whole(D_MODEL), _layer_spec((D_MODEL, D_MODEL), mix_layer)]
    args += [g, w_in, w_in, w_out, g_final]
    specs += [_layer_spec((1, D_MODEL), layer),
              pl.BlockSpec((None, D_MODEL, half), lambda c: (layer, 0, c)),
              pl.BlockSpec((None, D_MODEL, half), lambda c: (layer, 0, FFN_SMALL_SPLIT + c)),
              pl.BlockSpec((None, half, D_MODEL), lambda c: (layer, c, 0)),
              _const_spec((1, D_MODEL))]
    out_specs = [whole(D_MODEL)]
    out_shape = [jax.ShapeDtypeStruct((n, D_MODEL), F32)]
    kind = None
    if proj:
        kind, gmix, norm_layer = proj[:3]
        args.append(gmix)
        specs.append(_layer_spec((1, D_MODEL), norm_layer))
        if kind == "mlstm":
            w, wg, bg, mix_layer = proj[3:]
            args += [w, wg, bg]
            specs += [_layer_spec((D_MODEL, M_MAIN), mix_layer), _layer_spec((D_MODEL, LANES), mix_layer),
                      _layer_spec((1, LANES), mix_layer)]
            widths = [(M_MAIN, BF16), (LANES, F32)]
        else:
            wq, wkv, cos, sin_signed, mix_layer = proj[3:]
            args += [wq, wkv, cos, sin_signed]
            specs += [_layer_spec((D_MODEL, D_MODEL), mix_layer), _layer_spec((D_MODEL, 2 * A_KV), mix_layer),
                      whole(LANES), whole(LANES)]
            widths = [(D_MODEL, BF16), (2 * A_KV, F32)]
        out_specs += [whole(wd) for wd, _ in widths]
        out_shape += [jax.ShapeDtypeStruct((n, wd), dt) for wd, dt in widths]
    return pl.pallas_call(
        functools.partial(_ffn_small_body, mixer_out=bool(mixer_out), final=final, proj=kind),
        grid=(FFN_SMALL_SPLIT,),
        in_specs=specs,
        out_specs=out_specs,
        out_shape=out_shape,
        scratch_shapes=[pltpu.VMEM((n, D_MODEL), F32), pltpu.VMEM((n, D_MODEL), BF16),
                        pltpu.VMEM((n, D_MODEL), F32)],
        compiler_params=_params(1, 40),
        name="half_ffn_small",
    )(*args)


def _mlstm_body(qkvo_ref, gates_ref, c0_ref, n0_ref, m0_ref, hn_ref,
                hg_ref, c_ref, n_ref, m_ref, *, bg, L):
    @pl.when(pl.program_id(1) == 0)
    def _():
        c_ref[...] = c0_ref[...]
        n_ref[...] = n0_ref[...]
        m_ref[...] = m0_ref[...]

    _run(_mlstm_chunk([dict(qkvo=qkvo_ref.at[b], gates=gates_ref.at[b], hg=hg_ref.at[b],
                            c=c_ref.at[b], n=n_ref.at[b], m=m_ref.at[b]) for b in range(bg)],
                      hn_ref, L))


def _mlstm_chunk(seqs, hn_ref, L, keep=None):
    scale = M_DK ** -0.5
    row = lax.broadcasted_iota(jnp.int32, (L, L), 0)
    col = lax.broadcasted_iota(jnp.int32, (L, L), 1)
    causal = col <= row
    tril = causal.astype(BF16)
    lane = lax.broadcasted_iota(jnp.int32, (L, LANES), 1)

    probs = []
    for seq in seqs:
        gts = seq["gates"][...]
        lf = jax.nn.log_sigmoid(gts)
        lf_hi = lf.astype(BF16)
        r1 = lf - lf_hi.astype(F32)
        lf_mid = r1.astype(BF16)
        lf_lo = (r1 - lf_mid.astype(F32)).astype(BF16)
        bcs = (jnp.dot(tril, lf_hi, preferred_element_type=F32)
               + jnp.dot(tril, lf_mid, preferred_element_type=F32)
               + jnp.dot(tril, lf_lo, preferred_element_type=F32))
        both = jnp.where(lane < M_HEADS, gts, bcs)
        if L % LANES:
            both = jnp.concatenate([both, jnp.zeros((LANES - L, LANES), F32)], axis=0)
        both_t = both.T
        for h in range(M_HEADS):
            p = dict(seq=seq, h=h)
            b_row = both_t[M_HEADS + h:M_HEADS + h + 1, 0:L]
            p["b_last"] = b_row[:, L - 1:L]
            p["bmi_row"] = b_row - both_t[h:h + 1, 0:L]
            b_col = bcs[:, M_HEADS + h:M_HEADS + h + 1]
            p["b_rep"] = jnp.broadcast_to(b_col, (L, LANES))
            p["c_rep"] = jnp.broadcast_to(gts[:, h:h + 1] - b_col, (L, LANES))
            qkvo_ref = seq["qkvo"]
            p["m_prev"] = seq["m"][h:h + 1, 0:1]
            p["n_row"] = seq["n"][h:h + 1, :]
            p["c_prev"] = seq["c"][h]
            p["q"] = qkvo_ref[:, h * M_DK:(h + 1) * M_DK]
            p["k"] = qkvo_ref[:, M_QK + h * M_DK:M_QK + (h + 1) * M_DK]
            p["v"] = qkvo_ref[:, 2 * M_QK + h * M_DV:2 * M_QK + (h + 1) * M_DV]
            probs.append(p)

    def wide(c, width):
        return c[:, :width] if width <= LANES else jnp.concatenate([c] * (width // LANES), axis=1)

    def fold(x):
        acc = x[:, :LANES]
        for t in range(1, x.shape[1] // LANES):
            acc = acc + x[:, t * LANES:(t + 1) * LANES]
        return acc

    yield
    for p in probs:
        p["s"] = lax.dot_general(p["q"], p["k"], (((1,), (1,)), ((), ())), preferred_element_type=F32)
        p["qc"] = jnp.dot(p["q"], p["c_prev"].astype(BF16), preferred_element_type=F32)
        n_rep = jnp.broadcast_to(p["n_row"], (LANES, M_DK)).astype(BF16)
        p["qn"] = lax.dot_general(p["q"], n_rep, (((1,), (1,)), ((), ())), preferred_element_type=F32)

    yield
    for p in probs:
        d = jnp.where(causal, wide(p["b_rep"], L) - p["bmi_row"], -jnp.inf)
        g_rep = p["b_rep"] + p["m_prev"]
        m_t = jnp.maximum(g_rep, jnp.max(d, axis=-1, keepdims=True))
        w = jnp.exp(d - wide(m_t, L)) * (p["s"] * scale)
        p["m_t"] = m_t
        p["inter"] = jnp.exp(g_rep - m_t)
        p["w_sum"] = jnp.sum(fold(w), axis=-1, keepdims=True)
        p["w"] = w.astype(BF16)
        b_last = p["b_last"]
        a_row = b_last - p["bmi_row"]
        m_new = jnp.maximum(p["m_prev"] + b_last, jnp.max(a_row, axis=-1, keepdims=True))
        p["m_new"] = m_new
        p["decay"] = jnp.exp(p["m_prev"] + b_last - m_new)
        p["wk"] = (jnp.exp(b_last + p["c_rep"] - m_new) * scale) * p["k"].astype(F32)

    yield
    for p in probs:
        p["wv"] = jnp.dot(p["w"], p["v"], preferred_element_type=F32)
        p["kv"] = lax.dot_general(p["wk"].astype(BF16), p["v"], (((0,), (0,)), ((), ())),
                                  preferred_element_type=F32)

    def updated(new, old):
        return new if keep is None else jnp.where(keep, new, old)

    yield
    for p in probs:
        seq, h = p["seq"], p["h"]
        num = wide(p["inter"], M_DV) * p["qc"] + p["wv"]
        den = p["inter"] * p["qn"] + p["w_sum"]
        hh = num / wide(jnp.maximum(jnp.abs(den), jnp.exp(-p["m_t"])), M_DV)
        hh = hh * lax.rsqrt(jnp.mean(hh * hh, axis=-1, keepdims=True) + EPS)
        hh = hh * hn_ref[:, h * M_DV:(h + 1) * M_DV]
        og = seq["qkvo"][:, 2 * M_QK + M_V + h * M_DV:2 * M_QK + M_V + (h + 1) * M_DV]
        seq["hg"][:, h * M_DV:(h + 1) * M_DV] = (jax.nn.sigmoid(og.astype(F32)) * hh).astype(BF16)
        seq["c"][h] = updated(p["decay"] * p["c_prev"] + p["kv"], p["c_prev"])
        seq["n"][h:h + 1, :] = updated(
            p["decay"] * p["n_row"] + jnp.sum(p["wk"], axis=0, keepdims=True), p["n_row"])
        m_row = jnp.broadcast_to(p["m_new"], (1, LANES))
        seq["m"][h:h + 1, :] = updated(m_row, jnp.broadcast_to(p["m_prev"], (1, LANES)))


def _mlstm(qkvo, gates, c0, n0, m0, hnorm, *, layer, bg, L):
    nb, t, _ = qkvo.shape
    state_shapes = [(bg, M_HEADS, M_DK, M_DV), (bg, M_HEADS, M_DK), (bg, M_HEADS, LANES)]
    state_specs = [pl.BlockSpec(shp, lambda g, c, nd=len(shp): (g,) + (0,) * (nd - 1))
                   for shp in state_shapes]
    init_specs = [pl.BlockSpec((None, *shp), lambda g, c, nd=len(shp): (layer, g) + (0,) * (nd - 1))
                  for shp in state_shapes]
    return pl.pallas_call(
        functools.partial(_mlstm_body, bg=bg, L=L),
        grid=(nb // bg, t // L),
        in_specs=[
            pl.BlockSpec((bg, L, M_MAIN), lambda g, c: (g, c, 0)),
            pl.BlockSpec((bg, L, LANES), lambda g, c: (g, c, 0)),
            *init_specs,
            _layer_spec((1, M_V), layer),
        ],
        out_specs=[pl.BlockSpec((bg, L, M_V), lambda g, c: (g, c, 0)), *state_specs],
        out_shape=[
            jax.ShapeDtypeStruct((nb, t, M_V), BF16),
            jax.ShapeDtypeStruct((nb, M_HEADS, M_DK, M_DV), F32),
            jax.ShapeDtypeStruct((nb, M_HEADS, M_DK), F32),
            jax.ShapeDtypeStruct((nb, M_HEADS, LANES), F32),
        ],
        compiler_params=_params(2, 32),
        name="mlstm_scan",
    )(qkvo, gates, c0, n0, m0, hnorm)


def _attn_groups(sinks_ref, q_ref, o_ref, kv_scr, vt_scr, problems):
    per_slice = 8

    def pace(idx):
        return (idx + 1) % per_slice == 0

    scores = []
    for idx, (q_lo, q_rows, k_lo, k_rows, g, first_valid) in enumerate(problems):
        kb = kv_scr[k_lo:k_lo + k_rows, g * A_HD:(g + 1) * A_HD]
        qs = jnp.concatenate(
            [q_ref[q_lo:q_lo + q_rows, hd * A_HD:(hd + 1) * A_HD]
             for hd in range(A_GROUP * g, A_GROUP * (g + 1))], axis=0)
        scores.append(lax.dot_general(kb, qs, (((1,), (1,)), ((), ())), preferred_element_type=F32))
        if pace(idx):
            yield
    probs = []
    for idx, (s, (q_lo, q_rows, k_lo, k_rows, g, first_valid)) in enumerate(zip(scores, problems)):
        if first_valid is not None:
            key = lax.broadcasted_iota(jnp.int32, s.shape, 0)
            s = jnp.where(key >= first_valid, s, -jnp.inf)
        sk = jnp.concatenate([jnp.full((1, q_rows), sinks_ref[hd], F32)
                              for hd in range(A_GROUP * g, A_GROUP * (g + 1))], axis=1)
        m = jnp.maximum(jnp.max(s, axis=0, keepdims=True), sk)
        p = jnp.exp(s - m)
        den = jnp.sum(p, axis=0, keepdims=True) + jnp.exp(sk - m)
        probs.append((p.astype(BF16), den))
        if pace(idx):
            yield
    outs = []
    for idx, ((p, den), (q_lo, q_rows, k_lo, k_rows, g, first_valid)) in enumerate(zip(probs, problems)):
        if vt_scr is None:
            vb = kv_scr[k_lo:k_lo + k_rows, A_KV + g * A_HD:A_KV + (g + 1) * A_HD]
            o_t = lax.dot_general(vb, p, (((0,), (0,)), ((), ())), preferred_element_type=F32)
        else:
            o_t = jnp.dot(vt_scr[g * A_HD:(g + 1) * A_HD, k_lo:k_lo + k_rows], p,
                          preferred_element_type=F32)
        outs.append(o_t / den)
        if pace(idx):
            yield
    for idx, (o_t, (q_lo, q_rows, k_lo, k_rows, g, first_valid)) in enumerate(zip(outs, problems)):
        o = o_t.T
        for j in range(A_GROUP):
            hd = A_GROUP * g + j
            o_ref[q_lo:q_lo + q_rows, hd * A_HD:(hd + 1) * A_HD] = (
                o[j * q_rows:(j + 1) * q_rows].astype(BF16))
        if pace(idx):
            yield


def _attn_prompt_tile(sinks_ref, q_ref, kvp_ref, kvo_ref, o_ref, kv_scr, vt_scr, first_of_seq):
    rows = q_ref.shape[0]
    kv_scr[0:WINDOW, :] = kvp_ref[...].astype(BF16)
    kv_scr[WINDOW:WINDOW + rows, :] = kvo_ref[...].astype(BF16)
    vt_scr[:, 0:WINDOW] = kvp_ref[:, A_KV:2 * A_KV].T.astype(BF16)
    vt_scr[:, WINDOW:WINDOW + rows] = kvo_ref[:, A_KV:2 * A_KV].T.astype(BF16)
    band = WINDOW + CHUNK
    n_missing = jnp.where(first_of_seq, WINDOW, 0)
    problems = []
    for i in range(rows // CHUNK):
        first_valid = (n_missing - i * CHUNK) if i * CHUNK < WINDOW else None
        for g in range(A_KV_HEADS):
            problems.append((i * CHUNK, CHUNK, i * CHUNK, band, g, first_valid))
    yield
    yield from _attn_groups(sinks_ref, q_ref, o_ref, kv_scr, vt_scr, problems)


def _attn_sample_body(sinks_ref, q_ref, ck_ref, cv_ref, kv_ref, o_ref, kv_scr):
    rows = ck_ref.shape[1]
    t = q_ref.shape[1]
    kv_scr[0:rows, 0:A_KV] = ck_ref[0].astype(BF16)
    kv_scr[0:rows, A_KV:2 * A_KV] = cv_ref[0].astype(BF16)
    kv_scr[rows:rows + t, :] = kv_ref[0].astype(BF16)
    _run(_attn_groups(sinks_ref, q_ref.at[0], o_ref.at[0], kv_scr, None,
                      [(0, t, 0, rows + t, g, None) for g in range(A_KV_HEADS)]))


def _attn_sample(q, cache_k, cache_v, kv, sinks, *, layer):
    nb, t, _ = q.shape
    rows = cache_k.shape[2]
    return pl.pallas_call(
        _attn_sample_body,
        grid=(nb,),
        in_specs=[
            pl.BlockSpec(memory_space=pltpu.SMEM),
            pl.BlockSpec((1, t, D_MODEL), lambda b: (b, 0, 0)),
            pl.BlockSpec((None, 1, rows, A_KV), lambda b: (layer, b, 0, 0)),
            pl.BlockSpec((None, 1, rows, A_KV), lambda b: (layer, b, 0, 0)),
            pl.BlockSpec((1, t, 2 * A_KV), lambda b: (b, 0, 0)),
        ],
        out_specs=pl.BlockSpec((1, t, D_MODEL), lambda b: (b, 0, 0)),
        out_shape=jax.ShapeDtypeStruct((nb, t, D_MODEL), BF16),
        scratch_shapes=[pltpu.VMEM((rows + t, 2 * A_KV), BF16)],
        compiler_params=_params(1, 32),
        name="swa_sample_attn",
    )(sinks, q, cache_k, cache_v, kv)


def _rope_tables(pos):
    inv = ROPE_THETA ** (-jnp.arange(0, A_HD, 2, dtype=F32) / A_HD)
    ang = pos.astype(F32)[:, None] * inv[None, :]
    cos = jnp.cos(ang)
    sin = jnp.sin(ang)
    reps = LANES // A_HD
    return (jnp.concatenate([cos, cos] * reps, axis=-1),
            jnp.concatenate([-sin, sin] * reps, axis=-1))


def kernel(x_prompt, x_sample, state_mlstm_C, state_mlstm_n, state_mlstm_m, cache_swa_k, cache_swa_v,
           ffn_norm1, ffn_w_in1, ffn_w_out1, mix_norm, mlstm_w_in, mlstm_b_gates, mlstm_head_norm,
           mlstm_w_out, swa_w_qkv, swa_sinks, swa_w_out, ffn_norm2, ffn_w_in2, ffn_w_out2, final_norm):
    bp, tp, _ = x_prompt.shape
    bs, ts, _ = x_sample.shape
    tm_p = 512

    w_in1, w_out1 = ffn_w_in1.astype(BF16), ffn_w_out1.astype(BF16)
    w_in2, w_out2 = ffn_w_in2.astype(BF16), ffn_w_out2.astype(BF16)
    m_w_main = mlstm_w_in[:, :, :M_MAIN].astype(BF16)
    n_gates = 2 * M_HEADS
    m_w_gates = jnp.pad(mlstm_w_in[:, :, M_MAIN:], ((0, 0), (0, 0), (0, LANES - n_gates))).astype(BF16)
    m_b_gates = jnp.pad(mlstm_b_gates.astype(F32), ((0, 0), (0, LANES - n_gates)))[:, None, :]
    m_w_out = mlstm_w_out.astype(BF16)
    s_wq = swa_w_qkv[:, :, :D_MODEL].astype(BF16)
    s_wkv = swa_w_qkv[:, :, D_MODEL:].astype(BF16)
    s_w_out = swa_w_out.astype(BF16)
    sinks = swa_sinks.astype(F32)
    rows3 = lambda a: a.astype(F32)[:, None, :]
    g1, g2, gmix, hnorm = rows3(ffn_norm1), rows3(ffn_norm2), rows3(mix_norm), rows3(mlstm_head_norm)
    gf = final_norm.astype(F32)[None, :]

    cos_p, sin_p = _rope_tables(jnp.arange(tp))
    cos_s, sin_s = _rope_tables(PAST_LEN + jnp.arange(ts))
    cos_s, sin_s = jnp.tile(cos_s, (bs, 1)), jnp.tile(sin_s, (bs, 1))

    yp = x_prompt.reshape(bp * tp, D_MODEL)
    ys = x_sample.reshape(bs * ts, D_MODEL)
    c0_s = state_mlstm_C.astype(F32)
    n0_s = state_mlstm_n.astype(F32)
    m0_s = jnp.broadcast_to(state_mlstm_m.astype(F32)[..., None], state_mlstm_m.shape + (LANES,))
    rows = cache_swa_k.shape[2]
    ck = cache_swa_k.astype(F32).reshape(-1, bs, rows, A_KV)
    cv = cache_swa_v.astype(F32).reshape(-1, bs, rows, A_KV)

    p_c, p_n, p_m, p_k, p_v = [], [], [], [], []
    s_c, s_n, s_m, s_k, s_v = [], [], [], [], []
    for i in range(DEPTH):
        j = i // 2
        if i % 2 == 0:
            proj_p = proj_s = ("mlstm", gmix, i, m_w_main, m_w_gates, m_b_gates, j)
        else:
            proj_p = ("swa", gmix, i, s_wq, s_wkv, cos_p, sin_p, j)
            proj_s = ("swa", gmix, i, s_wq, s_wkv, cos_s, sin_s, j)
        yp, pa_p, pb_p = _ffn(yp, g1, w_in1, w_out1, gf, layer=i, tm=tm_p, proj=proj_p)
        ys, pa_s, pb_s = _ffn_small(ys, g1, w_in1, w_out1, gf, layer=i, proj=proj_s)
        if i % 2 == 0:
            mix_p = dict(scan=(pa_p, pb_p, hnorm, m_w_out, j, tp // tm_p))

            qkvo, gates = pa_s, pb_s
            hg, c, n, m = _mlstm(qkvo.reshape(bs, ts, M_MAIN), gates.reshape(bs, ts, LANES),
                                 c0_s, n0_s, m0_s, hnorm, layer=j, bg=bs,
                                 L=min(CHUNK, ts))
            mix_s = (hg.reshape(bs * ts, M_V), m_w_out, j)
            s_c.append(c); s_n.append(n); s_m.append(m[:, :, 0])
        else:
            q, kv = pa_p, pb_p
            kv3 = kv.reshape(bp, tp, 2 * A_KV)
            mix_p = dict(attn=(q, kv, sinks[j], s_w_out, j, tp // tm_p))
            keep = min(WINDOW, tp)
            p_k.append(kv3[:, tp - keep:, :A_KV].reshape(bp, keep, A_KV_HEADS, A_HD))
            p_v.append(kv3[:, tp - keep:, A_KV:].reshape(bp, keep, A_KV_HEADS, A_HD))

            q, kv = pa_s, pb_s
            kv3 = kv.reshape(bs, ts, 2 * A_KV)
            o = _attn_sample(q.reshape(bs, ts, D_MODEL), ck, cv, kv3, sinks[j], layer=j)
            mix_s = (o.reshape(bs * ts, D_MODEL), s_w_out, j)
            s_k.append(kv3[:, :, :A_KV].reshape(bs, ts, A_KV_HEADS, A_HD))
            s_v.append(kv3[:, :, A_KV:].reshape(bs, ts, A_KV_HEADS, A_HD))
        last = i == DEPTH - 1
        yp, *state = _ffn(yp, g2, w_in2, w_out2, gf, layer=i, tm=tm_p, final=last, **mix_p)
        if state:
            c, n, m = state
            p_c.append(c); p_n.append(n); p_m.append(m[:, :, 0])
        ys, = _ffn_small(ys, g2, w_in2, w_out2, gf, layer=i, final=last, mixer_out=mix_s)

    return (yp.reshape(bp, tp, D_MODEL), ys.reshape(bs, ts, D_MODEL),
            jnp.stack(p_c), jnp.stack(p_n), jnp.stack(p_m), jnp.stack(p_k), jnp.stack(p_v),
            jnp.stack(s_c), jnp.stack(s_n), jnp.stack(s_m), jnp.stack(s_k), jnp.stack(s_v))
```
